```python
import math
import jax, jax.numpy as jnp
from jax import lax
import numpy as np

D_MODEL = 2048
BATCH = 4
SEQ = 2048
DEPTH = 4
DEC_BATCH = 128
DEC_SEQ = 1
PAST_LEN = 16384
PAGE_SIZE = 128

EPS = 1e-6
FFN_DIM = 256 * ((8 * D_MODEL // 3 + 255) // 256)
N_BRANCH = 3
POOL_WINDOWS = (2, 4, 8, 16)
POOL_GROUPS = 4
POOL_WIDTH = D_MODEL // 4
POOL_GROUP_DIM = POOL_WIDTH // POOL_GROUPS
POOL_HIST = 15
GMLP_CHUNK = 128
GMLP_GROUPS = 4
GMLP_WIDTH = D_MODEL // 4
GMLP_GROUP_DIM = GMLP_WIDTH // GMLP_GROUPS
DN_HEADS = D_MODEL // 256
DN_DK = 128
DN_DV = 128
DN_QK_WIDTH = DN_HEADS * DN_DK
DN_V_WIDTH = DN_HEADS * DN_DV
DN_CHUNK = 64
CONV_W = 4
CONV_DIM = 2 * DN_QK_WIDTH + DN_V_WIDTH
IN_SIZES = (POOL_WIDTH, GMLP_WIDTH, GMLP_WIDTH, CONV_DIM, DN_V_WIDTH, DN_HEADS, DN_HEADS, N_BRANCH * D_MODEL)
IN_COLS = sum(IN_SIZES)

kernel_name = 'hybrid_pool_gmlp_gdn_macaron_step'


def rmsnorm(x, g):
    xf = x.astype(jnp.float32)
    xf = xf * lax.rsqrt(jnp.mean(xf * xf, axis=-1, keepdims=True) + EPS)
    return xf.astype(x.dtype) * g


def l2norm(x):
    return x * lax.rsqrt(jnp.sum(x * x, axis=-1, keepdims=True) + EPS)


def swiglu(h, wg, wu, wd):
    return (jax.nn.silu(h @ wg) * (h @ wu)) @ wd


def pool_mix(a, hist, pos0, w_grp, scale):
    B, L, _ = a.shape
    ext = jnp.concatenate([hist.astype(a.dtype), a], axis=1)
    cs = jnp.cumsum(ext.astype(jnp.float32), axis=1)
    cs = jnp.concatenate([jnp.zeros((B, 1, POOL_WIDTH), jnp.float32), cs], axis=1)
    pos = pos0 + jnp.arange(L)
    end = cs[:, POOL_HIST + 1:]
    diffs = []
    for gi, w in enumerate(POOL_WINDOWS):
        sl = slice(gi * POOL_GROUP_DIM, (gi + 1) * POOL_GROUP_DIM)
        start = cs[:, POOL_HIST + 1 - w:POOL_HIST + 1 - w + L, sl]
        cnt = jnp.minimum(pos + 1, w).astype(jnp.float32)[None, :, None]
        diffs.append((end[..., sl] - start) / cnt - a[..., sl].astype(jnp.float32))
    d = jnp.stack(diffs, axis=2).astype(a.dtype)
    y = jnp.einsum('blgc,gcd->blgd', d, w_grp).reshape(B, L, POOL_WIDTH) * scale
    return y, ext[:, -POOL_HIST:]


def gmlp_mix(gu, gv, norm_g, ws, b):
    B, L, _ = gu.shape
    u = jax.nn.gelu(gu)
    zv = rmsnorm(jax.nn.gelu(gv), norm_g)
    N = -(-L // GMLP_CHUNK)
    zc = jnp.pad(zv, ((0, 0), (0, N * GMLP_CHUNK - L), (0, 0)))
    zc = zc.reshape(B, N, GMLP_CHUNK, GMLP_GROUPS, GMLP_GROUP_DIM)
    causal = jnp.tril(jnp.ones((GMLP_CHUNK, GMLP_CHUNK), bool))
    wm = jnp.where(causal, ws, 0)
    s = jnp.einsum('gij,bnjgc->bnigc', wm, zc) + b.T[None, None, :, :, None]
    s = s.reshape(B, N * GMLP_CHUNK, GMLP_WIDTH)[:, :L]
    return u * s, zv


def gated_delta_rule(q, k, v, g, beta, S0):
    B, L, H, _ = q.shape
    C = DN_CHUNK
    N = -(-L // C)
    pad = N * C - L

    def prep(t):
        t = jnp.pad(t, [(0, 0), (0, pad)] + [(0, 0)] * (t.ndim - 2))
        t = t.reshape((B, N, C) + t.shape[2:])
        return jnp.moveaxis(t, 3, 1)

    qc, kc, vc, gc, bc = prep(q), prep(k), prep(v), prep(g), prep(beta)
    gcum = jnp.cumsum(gc, axis=-1)
    causal = jnp.tril(jnp.ones((C, C), bool))
    strict = jnp.tril(jnp.ones((C, C), bool), -1)
    diff = gcum[..., :, None] - gcum[..., None, :]
    decay = jnp.where(causal, jnp.exp(jnp.where(causal, diff, 0.0)), 0.0)
    kb = kc * bc[..., None]
    vb = vc * bc[..., None]
    M = jnp.where(strict, jnp.einsum('bhnid,bhnjd->bhnij', kb, kc) * decay, 0.0)
    eye = jnp.eye(C, dtype=jnp.float32)
    T = lax.linalg.triangular_solve(eye + M, jnp.broadcast_to(eye, M.shape), left_side=True,
                                    lower=True, unit_diagonal=True)
    u = jnp.einsum('bhnij,bhnje->bhnie', T, vb)
    w = jnp.einsum('bhnij,bhnjd->bhnid', T, kb * jnp.exp(gcum)[..., None])
    A = jnp.where(causal, jnp.einsum('bhnid,bhnjd->bhnij', qc, kc) * decay, 0.0)
    qg = qc * jnp.exp(gcum)[..., None]
    kg = kc * jnp.exp(gcum[..., -1:] - gcum)[..., None]
    glast = jnp.exp(gcum[..., -1])

    def step(S, xs):
        qg_i, kg_i, u_i, w_i, A_i, gl_i = xs
        v_new = u_i - jnp.einsum('bhcd,bhde->bhce', w_i, S)
        o = jnp.einsum('bhcd,bhde->bhce', qg_i, S) + jnp.einsum('bhij,bhje->bhie', A_i, v_new)
        S = S * gl_i[..., None, None] + jnp.einsum('bhcd,bhce->bhde', kg_i, v_new)
        return S, o

    xs = (jnp.moveaxis(qg, 2, 0), jnp.moveaxis(kg, 2, 0), jnp.moveaxis(u, 2, 0),
          jnp.moveaxis(w, 2, 0), jnp.moveaxis(A, 2, 0), jnp.moveaxis(glast, 2, 0))
    S, o = lax.scan(step, S0, xs)
    o = jnp.moveaxis(jnp.moveaxis(o, 0, 2), 1, 3)
    o = o.reshape(B, N * C, H, DN_DV)[:, :L]
    return o, S


def delta_mix(qkv, z, b_raw, a_raw, hist, S0, conv_w, a_log, dt_bias, onorm):
    B, L, _ = qkv.shape
    ext = jnp.concatenate([hist.astype(qkv.dtype), qkv], axis=1)
    conv = lax.conv_general_dilated(ext, conv_w[:, None, :].astype(ext.dtype), window_strides=(1,),
                                    padding='VALID', dimension_numbers=('NWC', 'WIO', 'NWC'),
                                    feature_group_count=CONV_DIM)
    act = jax.nn.silu(conv.astype(jnp.float32))
    q, k, v = jnp.split(act, [DN_QK_WIDTH, 2 * DN_QK_WIDTH], axis=-1)
    q = l2norm(q.reshape(B, L, DN_HEADS, DN_DK)) * (DN_DK ** -0.5)
    k = l2norm(k.reshape(B, L, DN_HEADS, DN_DK))
    v = v.reshape(B, L, DN_HEADS, DN_DV)
    beta = jax.nn.sigmoid(b_raw.astype(jnp.float32))
    g = -jnp.exp(a_log.astype(jnp.float32)) * jax.nn.softplus(a_raw.astype(jnp.float32) + dt_bias.astype(jnp.float32))
    o, S = gated_delta_rule(q, k, v, g, beta, S0.astype(jnp.float32))
    o = o * lax.rsqrt(jnp.mean(o * o, axis=-1, keepdims=True) + EPS) * onorm.astype(jnp.float32)
    o = o * jax.nn.silu(z.astype(jnp.float32).reshape(B, L, DN_HEADS, DN_DV))
    return o.reshape(B, L, DN_V_WIDTH).astype(qkv.dtype), ext[:, -(CONV_W - 1):], S.astype(S0.dtype)


def token_mix(n, hist_pool, hist_conv, S0, pos0, l, P):
    B, L, _ = n.shape
    proj = n @ P['w_in'][l]
    split_pts = np.cumsum(IN_SIZES)[:-1].tolist()
    a_pool, gu, gv, qkv, z, b_raw, a_raw, g_raw = jnp.split(proj, split_pts, axis=-1)
    y_a, new_pool = pool_mix(a_pool, hist_pool, pos0, P['pool_w'][l], P['pool_scale'][l])
    y_b, zv = gmlp_mix(gu, gv, P['gmlp_norm'][l], P['gmlp_ws'][l], P['gmlp_b'][l])
    y_c, new_conv, S = delta_mix(qkv, z, b_raw, a_raw, hist_conv, S0, P['dn_conv'][l],
                                 P['dn_a_log'][l], P['dn_dt_bias'][l], P['dn_onorm'][l])
    gates = jax.nn.sigmoid(g_raw + P['b_gate'][l]).reshape(B, L, N_BRANCH, D_MODEL)
    m = (gates[..., 0, :] * (y_a @ P['proj_a'][l])
         + gates[..., 1, :] * (y_b @ P['proj_b'][l])
         + gates[..., 2, :] * (y_c @ P['proj_c'][l]))
    return m @ P['w_o'][l], new_pool, new_conv, S, zv


def run_trunk(x, st_pool, st_conv, st_delta, pos0, P):
    pools, convs, deltas, vrows = [], [], [], []
    for l in range(DEPTH):
        x = x + 0.5 * swiglu(rmsnorm(x, P['ffn1_norm'][l]), P['ffn1_wg'][l], P['ffn1_wu'][l], P['ffn1_wd'][l])
        m, hp, hc, S, zv = token_mix(rmsnorm(x, P['mix_norm'][l]), st_pool[l], st_conv[l], st_delta[l], pos0, l, P)
        x = x + m
        x = x + 0.5 * swiglu(rmsnorm(x, P['ffn2_norm'][l]), P['ffn2_wg'][l], P['ffn2_wu'][l], P['ffn2_wd'][l])
        pools.append(hp)
        convs.append(hc)
        deltas.append(S)
        vrows.append(zv)
    y = rmsnorm(x, P['final_norm'])
    return y, jnp.stack(pools), jnp.stack(convs), jnp.stack(deltas), jnp.stack(vrows)


def setup_inputs(seed: int = 0) -> dict:
    key = jax.random.key(seed)
    ks = iter(jax.random.split(key, 32))

    def nrm(shape, scale):
        return jax.random.normal(next(ks), shape, jnp.float32) * scale

    def gain(shape):
        return 1.0 + 0.02 * jax.random.normal(next(ks), shape, jnp.float32)

    L = DEPTH
    x_prompt = nrm((BATCH, SEQ, D_MODEL), 1.0)
    x_sample = nrm((DEC_BATCH, DEC_SEQ, D_MODEL), 1.0)
    state_delta = nrm((L, DEC_BATCH, DN_HEADS, DN_DK, DN_DV), DN_DK ** -0.5)
    state_conv = nrm((L, DEC_BATCH, CONV_W - 1, CONV_DIM), 1.0)
    state_pool = nrm((L, DEC_BATCH, POOL_HIST, POOL_WIDTH), 1.0)
    ffn1_norm = gain((L, D_MODEL))
    ffn1_wg = nrm((L, D_MODEL, FFN_DIM), D_MODEL ** -0.5)
    ffn1_wu = nrm((L, D_MODEL, FFN_DIM), D_MODEL ** -0.5)
    ffn1_wd = nrm((L, FFN_DIM, D_MODEL), FFN_DIM ** -0.5)
    mix_norm = gain((L, D_MODEL))
    w_in = nrm((L, D_MODEL, IN_COLS), D_MODEL ** -0.5)
    b_gate = nrm((L, N_BRANCH * D_MODEL), 0.1)
    pool_w = nrm((L, POOL_GROUPS, POOL_GROUP_DIM, POOL_GROUP_DIM), POOL_GROUP_DIM ** -0.5)
    pool_scale = gain((L, POOL_WIDTH))
    gmlp_norm = gain((L, GMLP_WIDTH))
    gmlp_ws = nrm((L, GMLP_GROUPS, GMLP_CHUNK, GMLP_CHUNK), GMLP_CHUNK ** -0.5)
    gmlp_b = 1.0 + nrm((L, GMLP_GROUPS, GMLP_CHUNK), 0.1)
    dn_conv = nrm((L, CONV_W, CONV_DIM), CONV_W ** -0.5)
    dn_a_log = jnp.log(jax.random.uniform(next(ks), (L, DN_HEADS), jnp.float32, 1.0, 16.0))
    dt = jnp.exp(jax.random.uniform(next(ks), (L, DN_HEADS), jnp.float32, math.log(1e-3), math.log(1e-1)))
    dn_dt_bias = dt + jnp.log(-jnp.expm1(-dt))
    dn_onorm = gain((L, DN_DV))
    proj_a = nrm((L, POOL_WIDTH, D_MODEL), POOL_WIDTH ** -0.5)
    proj_b = nrm((L, GMLP_WIDTH, D_MODEL), GMLP_WIDTH ** -0.5)
    proj_c = nrm((L, DN_V_WIDTH, D_MODEL), DN_V_WIDTH ** -0.5)
    w_o = nrm((L, D_MODEL, D_MODEL), D_MODEL ** -0.5)
    ffn2_norm = gain((L, D_MODEL))
    ffn2_wg = nrm((L, D_MODEL, FFN_DIM), D_MODEL ** -0.5)
    ffn2_wu = nrm((L, D_MODEL, FFN_DIM), D_MODEL ** -0.5)
    ffn2_wd = nrm((L, FFN_DIM, D_MODEL), FFN_DIM ** -0.5)
    final_norm = gain((D_MODEL,))
    return {'x_prompt': x_prompt, 'x_sample': x_sample, 'state_delta': state_delta,
            'state_conv': state_conv, 'state_pool': state_pool,
            'ffn1_norm': ffn1_norm, 'ffn1_wg': ffn1_wg, 'ffn1_wu': ffn1_wu, 'ffn1_wd': ffn1_wd,
            'mix_norm': mix_norm, 'w_in': w_in, 'b_gate': b_gate,
            'pool_w': pool_w, 'pool_scale': pool_scale,
            'gmlp_norm': gmlp_norm, 'gmlp_ws': gmlp_ws, 'gmlp_b': gmlp_b,
            'dn_conv': dn_conv, 'dn_a_log': dn_a_log, 'dn_dt_bias': dn_dt_bias, 'dn_onorm': dn_onorm,
            'proj_a': proj_a, 'proj_b': proj_b, 'proj_c': proj_c, 'w_o': w_o,
            'ffn2_norm': ffn2_norm, 'ffn2_wg': ffn2_wg, 'ffn2_wu': ffn2_wu, 'ffn2_wd': ffn2_wd,
            'final_norm': final_norm}


def reference(x_prompt, x_sample, state_delta, state_conv, state_pool,
              ffn1_norm, ffn1_wg, ffn1_wu, ffn1_wd, mix_norm, w_in, b_gate,
              pool_w, pool_scale, gmlp_norm, gmlp_ws, gmlp_b,
              dn_conv, dn_a_log, dn_dt_bias, dn_onorm,
              proj_a, proj_b, proj_c, w_o,
              ffn2_norm, ffn2_wg, ffn2_wu, ffn2_wd, final_norm):
    P = {'ffn1_norm': ffn1_norm, 'ffn1_wg': ffn1_wg, 'ffn1_wu': ffn1_wu, 'ffn1_wd': ffn1_wd,
         'mix_norm': mix_norm, 'w_in': w_in, 'b_gate': b_gate,
         'pool_w': pool_w, 'pool_scale': pool_scale,
         'gmlp_norm': gmlp_norm, 'gmlp_ws': gmlp_ws, 'gmlp_b': gmlp_b,
         'dn_conv': dn_conv, 'dn_a_log': dn_a_log, 'dn_dt_bias': dn_dt_bias, 'dn_onorm': dn_onorm,
         'proj_a': proj_a, 'proj_b': proj_b, 'proj_c': proj_c, 'w_o': w_o,
         'ffn2_norm': ffn2_norm, 'ffn2_wg': ffn2_wg, 'ffn2_wu': ffn2_wu, 'ffn2_wd': ffn2_wd,
         'final_norm': final_norm}
    pool0 = jnp.zeros((DEPTH, BATCH, POOL_HIST, POOL_WIDTH), x_prompt.dtype)
    conv0 = jnp.zeros((DEPTH, BATCH, CONV_W - 1, CONV_DIM), x_prompt.dtype)
    delta0 = jnp.zeros((DEPTH, BATCH, DN_HEADS, DN_DK, DN_DV), state_delta.dtype)
    y_prompt, pool_p, conv_p, delta_p, _ = run_trunk(x_prompt, pool0, conv0, delta0, 0, P)
    y_sample, pool_s, conv_s, delta_s, gmlp_v_sample = run_trunk(x_sample, state_pool, state_conv, state_delta, PAST_LEN, P)
    return (y_prompt, y_sample, delta_p, delta_s, conv_p, conv_s, pool_p, pool_s, gmlp_v_sample)
```

```python
import functools

import jax
import jax.numpy as jnp
from jax import lax
from jax.experimental import pallas as pl
from jax.experimental.pallas import tpu as pltpu

F32 = jnp.float32
BF16 = jnp.bfloat16

D_MODEL = 2048
BATCH = 4
SEQ = 2048
DEPTH = 4
DEC_BATCH = 128
EPS = 1e-6
FFN_DIM = 5632
POOL_WINDOWS = (2, 4, 8, 16)
POOL_WIDTH = 512
POOL_HIST = 15
GMLP_CHUNK = 128
GMLP_WIDTH = 512
DN_HEADS = 8
DN_DK = 128
DN_DV = 128
DN_CHUNK = 64
CONV_W = 4
CONV_DIM = 3072
GROUP = 128

N_PROMPT = BATCH * SEQ
N_ROWS = N_PROMPT + DEC_BATCH

COL_GATE = 0
COL_QKV = 6144
COL_Z = 9216
COL_A = 10240
COL_GU = 10752
COL_GV = 11264
COL_BA = 11776
PROJ_COLS = 12288

ROW_TILE = 640
MERGE_TILE = 320
FFN_TILE = 512
PROJ_TILE = 1024
SAMPLE_BLOCK = 8
VMEM_LIMIT = 56 * 1024 * 1024


def _rmsnorm(x, g):
    ms = jnp.mean(x * x, axis=-1, keepdims=True)
    return x * lax.rsqrt(ms + EPS) * g


def _silu(x):
    return x * jax.nn.sigmoid(x)


def _softplus(x):
    return jnp.maximum(x, 0.0) + jnp.log1p(jnp.exp(-jnp.abs(x)))


def _bdot(a, b):
    return jnp.einsum('hmk,hkn->hmn', a.astype(BF16), b.astype(BF16),
                      preferred_element_type=F32)


def _bdot_nt(a, b):
    return jnp.einsum('hmk,hnk->hmn', a.astype(BF16), b.astype(BF16),
                      preferred_element_type=F32)


def _bdot_tn(a, b):
    return jnp.einsum('hcm,hcn->hmn', a.astype(BF16), b.astype(BF16),
                      preferred_element_type=F32)


def _split(x):
    hi = x.astype(BF16)
    lo = (x - hi.astype(F32)).astype(BF16)
    return hi, lo


def _bdot3(a, b):
    ah, al = _split(a)
    bh, bl = _split(b)
    return _bdot(ah, bh) + (_bdot(ah, bl) + _bdot(al, bh))


def _ffn_kernel(x_ref, g_ref, wg_ref, wu_ref, wd_ref, o_ref, h_ref):
    @pl.when(pl.program_id(1) == 0)
    def _():
        x = x_ref[...]
        h_ref[...] = _rmsnorm(x, g_ref[...]).astype(BF16)
        o_ref[...] = x

    h = h_ref[...]
    a = jnp.dot(h, wg_ref[...], preferred_element_type=F32)
    b = jnp.dot(h, wu_ref[...], preferred_element_type=F32)
    act = (_silu(a) * (0.5 * b)).astype(BF16)
    o_ref[...] += jnp.dot(act, wd_ref[...], preferred_element_type=F32)


def _ffn(x, g, wg, wu, wd):
    rows = x.shape[0]
    return pl.pallas_call(
        _ffn_kernel,
        grid=(rows // ROW_TILE, FFN_DIM // FFN_TILE),
        in_specs=[
            pl.BlockSpec((ROW_TILE, D_MODEL), lambda i, j: (i, 0)),
            pl.BlockSpec((1, D_MODEL), lambda i, j: (0, 0)),
            pl.BlockSpec((D_MODEL, FFN_TILE), lambda i, j: (0, j)),
            pl.BlockSpec((D_MODEL, FFN_TILE), lambda i, j: (0, j)),
            pl.BlockSpec((FFN_TILE, D_MODEL), lambda i, j: (j, 0)),
        ],
        out_specs=pl.BlockSpec((ROW_TILE, D_MODEL), lambda i, j: (i, 0)),
        out_shape=jax.ShapeDtypeStruct((rows, D_MODEL), F32),
        scratch_shapes=[pltpu.VMEM((ROW_TILE, D_MODEL), BF16)],
        compiler_params=pltpu.CompilerParams(
            dimension_semantics=("parallel", "arbitrary"),
            vmem_limit_bytes=VMEM_LIMIT),
        name="ffn",
    )(x, g, wg, wu, wd)


def _inproj_kernel(x_ref, g_ref, w_ref, o_ref, h_ref):
    @pl.when(pl.program_id(1) == 0)
    def _():
        h_ref[...] = _rmsnorm(x_ref[...], g_ref[...]).astype(BF16)

    o_ref[...] = jnp.dot(h_ref[...], w_ref[...], preferred_element_type=F32)


def _inproj(x, g, w):
    rows = x.shape[0]
    return pl.pallas_call(
        _inproj_kernel,
        grid=(rows // ROW_TILE, PROJ_COLS // PROJ_TILE),
        in_specs=[
            pl.BlockSpec((ROW_TILE, D_MODEL), lambda i, j: (i, 0)),
            pl.BlockSpec((1, D_MODEL), lambda i, j: (0, 0)),
            pl.BlockSpec((D_MODEL, PROJ_TILE), lambda i, j: (0, j)),
        ],
        out_specs=pl.BlockSpec((ROW_TILE, PROJ_TILE), lambda i, j: (i, j)),
        out_shape=jax.ShapeDtypeStruct((rows, PROJ_COLS), F32),
        scratch_shapes=[pltpu.VMEM((ROW_TILE, D_MODEL), BF16)],
        compiler_params=pltpu.CompilerParams(
            dimension_semantics=("parallel", "arbitrary"),
            vmem_limit_bytes=VMEM_LIMIT),
        name="inproj",
    )(x, g, w)


def _gmlp_gate(gu, gv, norm_g):
    u = jax.nn.gelu(gu)
    zv = _rmsnorm(jax.nn.gelu(gv), norm_g)
    return u, zv


def _pool_gmlp_kernel(a_ref, gu_ref, gv_ref, pw_ref, ps_ref, gn_ref, ws_ref, gb_ref,
                      ya_ref, yb_ref, ext_ref):
    c = pl.program_id(1)
    hist = 16
    rows = GMLP_CHUNK

    @pl.when(c == 0)
    def _():
        ext_ref[0:hist, :] = jnp.zeros((hist, POOL_WIDTH), F32)

    a = a_ref[...]
    ext_ref[hist:hist + rows, :] = a
    pos = c * rows + lax.broadcasted_iota(jnp.int32, (rows, 1), 0)
    for gi, w in enumerate(POOL_WINDOWS):
        sl = slice(gi * GROUP, (gi + 1) * GROUP)
        s = a[:, sl]
        for i in range(1, w):
            s = s + ext_ref[hist - i:hist - i + rows, sl]
        cnt = jnp.minimum(pos + 1, w).astype(F32)
        d = s / cnt - a[:, sl]
        y = jnp.dot(d.astype(BF16), pw_ref[gi], preferred_element_type=F32)
        ya_ref[:, sl] = y * ps_ref[:, sl]
    ext_ref[0:hist, :] = a[rows - hist:rows, :]

    u, zv = _gmlp_gate(gu_ref[...], gv_ref[...], gn_ref[...])
    for gi in range(4):
        sl = slice(gi * GROUP, (gi + 1) * GROUP)
        s = jnp.dot(ws_ref[gi], zv[:, sl].astype(BF16), preferred_element_type=F32)
        yb_ref[:, sl] = u[:, sl] * (s + gb_ref[:, sl])


def _pool_gmlp(proj, pool_w, pool_scale, gmlp_norm, ws_tril, gb_full):
    chunks = SEQ // GMLP_CHUNK
    row = lambda b, c: b * chunks + c
    full = lambda shape: pl.BlockSpec(shape, lambda b, c: (0,) * len(shape))
    return pl.pallas_call(
        _pool_gmlp_kernel,
        grid=(BATCH, chunks),
        in_specs=[
            pl.BlockSpec((GMLP_CHUNK, POOL_WIDTH), lambda b, c: (row(b, c), COL_A // POOL_WIDTH)),
            pl.BlockSpec((GMLP_CHUNK, GMLP_WIDTH), lambda b, c: (row(b, c), COL_GU // GMLP_WIDTH)),
            pl.BlockSpec((GMLP_CHUNK, GMLP_WIDTH), lambda b, c: (row(b, c), COL_GV // GMLP_WIDTH)),
            full((4, GROUP, GROUP)),
            full((1, POOL_WIDTH)),
            full((1, GMLP_WIDTH)),
            full((4, GMLP_CHUNK, GMLP_CHUNK)),
            full((GMLP_CHUNK, GMLP_WIDTH)),
        ],
        out_specs=[
            pl.BlockSpec((GMLP_CHUNK, POOL_WIDTH), lambda b, c: (row(b, c), 0)),
            pl.BlockSpec((GMLP_CHUNK, GMLP_WIDTH), lambda b, c: (row(b, c), 0)),
        ],
        out_shape=[
            jax.ShapeDtypeStruct((N_PROMPT, POOL_WIDTH), F32),
            jax.ShapeDtypeStruct((N_PROMPT, GMLP_WIDTH), F32),
        ],
        scratch_shapes=[pltpu.VMEM((16 + GMLP_CHUNK, POOL_WIDTH), F32)],
        compiler_params=pltpu.CompilerParams(
            dimension_semantics=("parallel", "arbitrary"),
            vmem_limit_bytes=VMEM_LIMIT),
        name="pool_gmlp",
    )(proj, proj, proj, pool_w, pool_scale, gmlp_norm, ws_tril, gb_full)


def _tri_inverse(m, ri, ci):
    blk = (ri // 16) == (ci // 16)
    eye = jnp.where(ri == ci, 1.0, 0.0).astype(F32)
    d = jnp.where(blk, m, 0.0)
    low = m - d
    d2 = _bdot3(d, d)
    d4 = _bdot3(d2, d2)
    d8 = _bdot3(d4, d4)
    td = eye - d
    td = td + _bdot3(td, d2)
    td = td + _bdot3(td, d4)
    td = td + _bdot3(td, d8)
    n = _bdot3(td, low)
    n2 = _bdot3(n, n)
    x = eye - n + n2 - _bdot3(n, n2)
    return _bdot3(x, td)


def _cumsum_exact(g, tri, dims):
    g1 = g.astype(BF16)
    r1 = g - g1.astype(F32)
    g2 = r1.astype(BF16)
    g3 = (r1 - g2.astype(F32)).astype(BF16)
    if dims == 'rows':
        dot = lambda t: jnp.dot(tri, t, preferred_element_type=F32)
    else:
        dot = lambda t: jnp.dot(t, tri, preferred_element_type=F32)
    return dot(g1) + (dot(g2) + dot(g3))


def _delta_kernel(qkv_ref, z_ref, ba_ref, bat_ref, cw_ref, acol_ref, dtcol_ref,
                  arow_ref, dtrow_ref, on_ref, yc_ref, s_ref, ext_ref):
    n = pl.program_id(1)
    C = DN_CHUNK
    H = DN_HEADS

    @pl.when(n == 0)
    def _():
        ext_ref[0:8, :] = jnp.zeros((8, CONV_DIM), F32)
        s_ref[...] = jnp.zeros(s_ref.shape, F32)

    x = qkv_ref[...]
    ext_ref[8:8 + C, :] = x
    conv = cw_ref[3:4, :] * x
    for i in range(CONV_W - 1):
        conv = conv + cw_ref[i:i + 1, :] * ext_ref[5 + i:5 + i + C, :]
    ext_ref[0:8, :] = x[C - 8:C, :]
    act = _silu(conv)

    heads = lambda base: jnp.stack(
        [act[:, base + h * DN_DK:base + (h + 1) * DN_DK] for h in range(H)])
    q = heads(0)
    k = heads(H * DN_DK)
    v = heads(2 * H * DN_DK)
    q = q * lax.rsqrt(jnp.sum(q * q, axis=-1, keepdims=True) + EPS) * (DN_DK ** -0.5)
    k = k * lax.rsqrt(jnp.sum(k * k, axis=-1, keepdims=True) + EPS)

    ri = lax.broadcasted_iota(jnp.int32, (C, C), 0)
    ci = lax.broadcasted_iota(jnp.int32, (C, C), 1)
    causal = ri >= ci
    strict = ri > ci
    tril = jnp.where(causal, 1.0, 0.0).astype(BF16)
    triu = jnp.where(ri <= ci, 1.0, 0.0).astype(BF16)

    ba = ba_ref[...]
    beta_all = jax.nn.sigmoid(ba)
    g_all = -jnp.exp(acol_ref[...]) * _softplus(ba + dtcol_ref[...])
    gc_all = _cumsum_exact(g_all, tril, 'rows')
    beta = jnp.stack([beta_all[:, h:h + 1] for h in range(H)])
    gc = jnp.stack([gc_all[:, H + h:H + h + 1] for h in range(H)])
    bat = bat_ref[0]
    g_t = -jnp.exp(arow_ref[...]) * _softplus(bat + dtrow_ref[...])
    gc_t = _cumsum_exact(g_t, triu, 'cols')
    gr = jnp.stack([gc_t[H + h:H + h + 1, :] for h in range(H)])

    diff = gc - gr
    decay = jnp.where(causal, jnp.exp(jnp.where(causal, diff, 0.0)), 0.0)
    egc = jnp.exp(gc)
    glast = gc[:, C - 1:C, :]
    kb = k * beta
    vb = v * beta
    m = jnp.where(strict, _bdot_nt(kb, k) * decay, 0.0)
    a_mat = _bdot_nt(q, k) * decay
    t = _tri_inverse(m, ri, ci)
    uw = _bdot(t, jnp.concatenate([vb, kb * egc], axis=-1))
    u = uw[:, :, :DN_DV]
    w = uw[:, :, DN_DV:]

    s = s_ref[0]
    s16 = s.astype(BF16)
    v_new = u - _bdot(w, s16)
    o = _bdot(q * egc, s16) + _bdot(a_mat, v_new)
    kg = k * jnp.exp(glast - gc)
    s_ref[0] = s * jnp.exp(glast) + _bdot_tn(kg, v_new)

    o = o * lax.rsqrt(jnp.mean(o * o, axis=-1, keepdims=True) + EPS) * on_ref[...]
    z = z_ref[...]
    for h in range(H):
        sl = slice(h * DN_DV, (h + 1) * DN_DV)
        yc_ref[:, sl] = o[h] * _silu(z[:, sl])


def _delta(proj, bat, conv_w, acol, dtcol, arow, dtrow, onorm):
    chunks = SEQ // DN_CHUNK
    row = lambda b, n: b * chunks + n
    full = lambda shape: pl.BlockSpec(shape, lambda b, n: (0,) * len(shape))
    return pl.pallas_call(
        _delta_kernel,
        grid=(BATCH, chunks),
        in_specs=[
            pl.BlockSpec((DN_CHUNK, CONV_DIM), lambda b, n: (row(b, n), COL_QKV // CONV_DIM)),
            pl.BlockSpec((DN_CHUNK, 1024), lambda b, n: (row(b, n), COL_Z // 1024)),
            pl.BlockSpec((DN_CHUNK, 128), lambda b, n: (row(b, n), COL_BA // 128)),
            pl.BlockSpec((1, 16, DN_CHUNK), lambda b, n: (row(b, n), 0, 0)),
            full((CONV_W, CONV_DIM)),
            full((1, 128)),
            full((1, 128)),
            full((16, 1)),
            full((16, 1)),
            full((1, DN_DV)),
        ],
        out_specs=[
            pl.BlockSpec((DN_CHUNK, 1024), lambda b, n: (row(b, n), 0)),
            pl.BlockSpec((1, DN_HEADS, DN_DK, DN_DV), lambda b, n: (b, 0, 0, 0)),
        ],
        out_shape=[
            jax.ShapeDtypeStruct((N_PROMPT, 1024), F32),
            jax.ShapeDtypeStruct((BATCH, DN_HEADS, DN_DK, DN_DV), F32),
        ],
        scratch_shapes=[pltpu.VMEM((8 + DN_CHUNK, CONV_DIM), F32)],
        compiler_params=pltpu.CompilerParams(
            dimension_semantics=("parallel", "arbitrary"),
            vmem_limit_bytes=VMEM_LIMIT),
        name="delta",
    )(proj, proj, proj, bat, conv_w, acol, dtcol, arow, dtrow, onorm)


def _sample_rows_kernel(a_ref, gu_ref, gv_ref, qkv_ref, ba_ref, pool_ref, cst_ref,
                        pw_ref, ps_ref, gn_ref, wsd_ref, gb0_ref, cw_ref, acol_ref, dtcol_ref,
                        ya_ref, yb_ref, zv_ref, npool_ref, nconv_ref, vec_ref, eg_ref, qk_ref):
    H = DN_HEADS
    a = a_ref[...]
    for gi, w in enumerate(POOL_WINDOWS):
        sl = slice(gi * GROUP, (gi + 1) * GROUP)
        s = a[:, sl]
        for i in range(1, w):
            s = s + pool_ref[POOL_HIST - i, :, sl]
        d = s / float(w) - a[:, sl]
        y = jnp.dot(d.astype(BF16), pw_ref[gi], preferred_element_type=F32)
        ya_ref[:, sl] = y * ps_ref[:, sl]
    for i in range(POOL_HIST - 1):
        npool_ref[i] = pool_ref[i + 1]
    npool_ref[POOL_HIST - 1] = a

    u, zv = _gmlp_gate(gu_ref[...], gv_ref[...], gn_ref[...])
    zv_ref[...] = zv
    yb_ref[...] = u * (wsd_ref[...] * zv + gb0_ref[...])

    x = qkv_ref[...]
    conv = cw_ref[3:4, :] * x
    for i in range(CONV_W - 1):
        conv = conv + cw_ref[i:i + 1, :] * cst_ref[i]
    nconv_ref[0] = cst_ref[1]
    nconv_ref[1] = cst_ref[2]
    nconv_ref[2] = x
    act = _silu(conv)
    ba = ba_ref[...]
    beta_all = jax.nn.sigmoid(ba)
    g_all = -jnp.exp(acol_ref[...]) * _softplus(ba + dtcol_ref[...])
    eg_all = jnp.exp(g_all)
    eg_ref[...] = eg_all
    lane = lax.broadcasted_iota(jnp.int32, (DEC_BATCH, 128), 1)
    qk_all = jnp.zeros((DEC_BATCH, 128), F32)
    for h in range(H):
        q = act[:, h * DN_DK:(h + 1) * DN_DK]
        k = act[:, (H + h) * DN_DK:(H + h + 1) * DN_DK]
        v = act[:, (2 * H + h) * DN_DK:(2 * H + h + 1) * DN_DK]
        q = q * lax.rsqrt(jnp.sum(q * q, axis=-1, keepdims=True) + EPS) * (DN_DK ** -0.5)
        k = k * lax.rsqrt(jnp.sum(k * k, axis=-1, keepdims=True) + EPS)
        beta = beta_all[:, h:h + 1]
        eg = eg_all[:, H + h:H + h + 1]
        vec_ref[:, h * 128:(h + 1) * 128] = k * (beta * eg)
        vec_ref[:, (H + h) * 128:(H + h + 1) * 128] = q * eg
        vec_ref[:, (2 * H + h) * 128:(2 * H + h + 1) * 128] = k
        vec_ref[:, (3 * H + h) * 128:(3 * H + h + 1) * 128] = v * beta
        qk = jnp.sum(q * k, axis=-1, keepdims=True)
        qk_all = jnp.where(lane == h, qk, qk_all)
    qk_ref[...] = qk_all


def _sample_rows(proj, pool_t, conv_t, pool_w, pool_scale, gmlp_norm, ws_diag, gb0, conv_w, acol, dtcol):
    rb = N_PROMPT // DEC_BATCH
    full = lambda shape: pl.BlockSpec(shape, lambda i: (0,) * len(shape))
    n = DEC_BATCH
    return pl.pallas_call(
        _sample_rows_kernel,
        grid=(1,),
        in_specs=[
            pl.BlockSpec((n, POOL_WIDTH), lambda i: (rb, COL_A // POOL_WIDTH)),
            pl.BlockSpec((n, GMLP_WIDTH), lambda i: (rb, COL_GU // GMLP_WIDTH)),
            pl.BlockSpec((n, GMLP_WIDTH), lambda i: (rb, COL_GV // GMLP_WIDTH)),
            pl.BlockSpec((n, CONV_DIM), lambda i: (rb, COL_QKV // CONV_DIM)),
            pl.BlockSpec((n, 128), lambda i: (rb, COL_BA // 128)),
            full((POOL_HIST, n, POOL_WIDTH)),
            full((CONV_W - 1, n, CONV_DIM)),
            full((4, GROUP, GROUP)),
            full((1, POOL_WIDTH)),
            full((1, GMLP_WIDTH)),
            full((1, GMLP_WIDTH)),
            full((1, GMLP_WIDTH)),
            full((CONV_W, CONV_DIM)),
            full((1, 128)),
            full((1, 128)),
        ],
        out_specs=[
            full((n, POOL_WIDTH)),
            full((n, GMLP_WIDTH)),
            full((n, GMLP_WIDTH)),
            full((POOL_HIST, n, POOL_WIDTH)),
            full((CONV_W - 1, n, CONV_DIM)),
            full((n, 4 * DN_HEADS * 128)),
            full((n, 128)),
            full((n, 128)),
        ],
        out_shape=[
            jax.ShapeDtypeStruct((n, POOL_WIDTH), F32),
            jax.ShapeDtypeStruct((n, GMLP_WIDTH), F32),
            jax.ShapeDtypeStruct((n, GMLP_WIDTH), F32),
            jax.ShapeDtypeStruct((POOL_HIST, n, POOL_WIDTH), F32),
            jax.ShapeDtypeStruct((CONV_W - 1, n, CONV_DIM), F32),
            jax.ShapeDtypeStruct((n, 4 * DN_HEADS * 128), F32),
            jax.ShapeDtypeStruct((n, 128), F32),
            jax.ShapeDtypeStruct((n, 128), F32),
        ],
        compiler_params=pltpu.CompilerParams(
            dimension_semantics=("arbitrary",),
            vmem_limit_bytes=VMEM_LIMIT),
        name="sample_rows",
    )(proj, proj, proj, proj, proj, pool_t, conv_t, pool_w, pool_scale, gmlp_norm,
      ws_diag, gb0, conv_w, acol, dtcol)


def _sample_state_kernel(vec_ref, scal_ref, z_ref, on_ref, s_ref, yc_ref, so_ref):
    H = DN_HEADS
    row = lax.broadcasted_iota(jnp.int32, (1, 8, 1), 1)
    for h in range(H):
        s = s_ref[:, h]
        w = vec_ref[:, h:h + 1, :]
        qg = vec_ref[:, H + h:H + h + 1, :]
        k = vec_ref[:, 2 * H + h:2 * H + h + 1, :]
        vb = vec_ref[:, 3 * H + h:3 * H + h + 1, :]
        eg = scal_ref[:, h:h + 1, :]
        qk = scal_ref[:, H + h:H + h + 1, :]
        lhs = jnp.where(row == 0, w, jnp.where(row == 1, qg, 0.0))
        r = _bdot(lhs, s)
        v_new = vb - r[:, 0:1, :]
        o = r[:, 1:2, :] + qk * v_new
        k8 = jnp.where(row == 0, k, 0.0)
        v8 = jnp.broadcast_to(v_new, k8.shape)
        so_ref[:, h] = s * eg + _bdot_tn(k8, v8)
        o = o * lax.rsqrt(jnp.mean(o * o, axis=-1, keepdims=True) + EPS) * on_ref[...]
        yc_ref[:, h:h + 1, :] = o * _silu(z_ref[:, h:h + 1, :])


def _sample_state(vec3, scal3, z3, onorm, state):
    sb = SAMPLE_BLOCK
    H = DN_HEADS
    return pl.pallas_call(
        _sample_state_kernel,
        grid=(DEC_BATCH // sb,),
        in_specs=[
            pl.BlockSpec((sb, 4 * H, 128), lambda i: (i, 0, 0)),
            pl.BlockSpec((sb, 2 * H, 1), lambda i: (i, 0, 0)),
            pl.BlockSpec((sb, H, DN_DV), lambda i: (i, 0, 0)),
            pl.BlockSpec((1, DN_DV), lambda i: (0, 0)),
            pl.BlockSpec((sb, H, DN_DK, DN_DV), lambda i: (i, 0, 0, 0)),
        ],
        out_specs=[
            pl.BlockSpec((sb, H, DN_DV), lambda i: (i, 0, 0)),
            pl.BlockSpec((sb, H, DN_DK, DN_DV), lambda i: (i, 0, 0, 0)),
        ],
        out_shape=[
            jax.ShapeDtypeStruct((DEC_BATCH, H, DN_DV), F32),
            jax.ShapeDtypeStruct((DEC_BATCH, H, DN_DK, DN_DV), F32),
        ],
        compiler_params=pltpu.CompilerParams(
            dimension_semantics=("parallel",),
            vmem_limit_bytes=VMEM_LIMIT),
        name="sample_state",
    )(vec3, scal3, z3, onorm, state)


def _merge_kernel(x_ref, g0_ref, g1_ref, g2_ref, bg_ref, ya_ref, yb_ref, yc_ref,
                  pa_ref, pb_ref, pc_ref, wo_ref, o_ref):
    def branch(g_ref, i, y_ref, p_ref):
        gate = jax.nn.sigmoid(g_ref[...] + bg_ref[:, i * D_MODEL:(i + 1) * D_MODEL])
        return gate * jnp.dot(y_ref[...].astype(BF16), p_ref[...], preferred_element_type=F32)

    m = branch(g0_ref, 0, ya_ref, pa_ref)
    m = m + branch(g1_ref, 1, yb_ref, pb_ref)
    m = m + branch(g2_ref, 2, yc_ref, pc_ref)
    o_ref[...] = x_ref[...] + jnp.dot(m.astype(BF16), wo_ref[...], preferred_element_type=F32)


def _merge(x, proj, b_gate, ya, yb, yc, pa, pb, pc, wo):
    rows = x.shape[0]
    tm = MERGE_TILE
    once = lambda shape: pl.BlockSpec(shape, lambda i: (0,) * len(shape),
                                      pipeline_mode=pl.Buffered(1))
    return pl.pallas_call(
        _merge_kernel,
        grid=(rows // tm,),
        in_specs=[
            pl.BlockSpec((tm, D_MODEL), lambda i: (i, 0)),
            pl.BlockSpec((tm, D_MODEL), lambda i: (i, 0)),
            pl.BlockSpec((tm, D_MODEL), lambda i: (i, 1)),
            pl.BlockSpec((tm, D_MODEL), lambda i: (i, 2)),
            once((1, 3 * D_MODEL)),
            pl.BlockSpec((tm, POOL_WIDTH), lambda i: (i, 0)),
            pl.BlockSpec((tm, GMLP_WIDTH), lambda i: (i, 0)),
            pl.BlockSpec((tm, 1024), lambda i: (i, 0)),
            once((POOL_WIDTH, D_MODEL)),
            once((GMLP_WIDTH, D_MODEL)),
            once((1024, D_MODEL)),
            once((D_MODEL, D_MODEL)),
        ],
        out_specs=pl.BlockSpec((tm, D_MODEL), lambda i: (i, 0)),
        out_shape=jax.ShapeDtypeStruct((rows, D_MODEL), F32),
        compiler_params=pltpu.CompilerParams(
            dimension_semantics=("parallel",),
            vmem_limit_bytes=VMEM_LIMIT),
        name="merge",
    )(x, proj, proj, proj, b_gate, ya, yb, yc, pa, pb, pc, wo)


def _final_kernel(x_ref, g_ref, o_ref):
    o_ref[...] = _rmsnorm(x_ref[...], g_ref[...])


def _final_norm(x, g):
    rows = x.shape[0]
    return pl.pallas_call(
        _final_kernel,
        grid=(rows // ROW_TILE,),
        in_specs=[pl.BlockSpec((ROW_TILE, D_MODEL), lambda i: (i, 0)),
                  pl.BlockSpec((1, D_MODEL), lambda i: (0, 0))],
        out_specs=pl.BlockSpec((ROW_TILE, D_MODEL), lambda i: (i, 0)),
        out_shape=jax.ShapeDtypeStruct((rows, D_MODEL), F32),
        compiler_params=pltpu.CompilerParams(dimension_semantics=("parallel",),
                                             vmem_limit_bytes=VMEM_LIMIT),
        name="final_norm",
    )(x, g)


def _lane_pad(v, offset, width=128):
    return jnp.zeros((1, width), F32).at[0, offset:offset + v.shape[0]].set(v)


def kernel(x_prompt, x_sample, state_delta, state_conv, state_pool, ffn1_norm, ffn1_wg, ffn1_wu, ffn1_wd, mix_norm, w_in, b_gate, pool_w, pool_scale, gmlp_norm, gmlp_ws, gmlp_b, dn_conv, dn_a_log, dn_dt_bias, dn_onorm, proj_a, proj_b, proj_c, w_o, ffn2_norm, ffn2_wg, ffn2_wu, ffn2_wd, final_norm):
    H = DN_HEADS
    x = jnp.concatenate([x_prompt.reshape(N_PROMPT, D_MODEL),
                         x_sample.reshape(DEC_BATCH, D_MODEL)], axis=0)
    causal = jnp.tril(jnp.ones((GMLP_CHUNK, GMLP_CHUNK), bool))
    outs_delta_p, outs_delta_s, outs_conv_p, outs_conv_s = [], [], [], []
    outs_pool_p, outs_pool_s, outs_zv = [], [], []
    for l in range(DEPTH):
        x = _ffn(x, ffn1_norm[l][None], ffn1_wg[l].astype(BF16), ffn1_wu[l].astype(BF16),
                 ffn1_wd[l].astype(BF16))

        wl = w_in[l]
        w_r = jnp.concatenate([
            wl[:, 5648:11792], wl[:, 1536:4608], wl[:, 4608:5632], wl[:, 0:1536],
            wl[:, 5632:5648], jnp.zeros((D_MODEL, PROJ_COLS - 11792), F32)], axis=1).astype(BF16)
        proj = _inproj(x, mix_norm[l][None], w_r)

        pw = pool_w[l].astype(BF16)
        ps = pool_scale[l][None]
        gn = gmlp_norm[l][None]
        ws_tril = jnp.where(causal, gmlp_ws[l], 0).astype(BF16)
        gb_full = jnp.repeat(gmlp_b[l].T, GROUP, axis=1)
        ws_diag = jnp.repeat(gmlp_ws[l][:, 0, 0], GROUP)[None]
        gb0 = jnp.repeat(gmlp_b[l][:, 0], GROUP)[None]
        acol = _lane_pad(dn_a_log[l], H)
        dtcol = _lane_pad(dn_dt_bias[l], H)
        arow = acol[0, :16][:, None]
        dtrow = dtcol[0, :16][:, None]
        onorm = dn_onorm[l][None]

        ya_p, yb_p = _pool_gmlp(proj, pw, ps, gn, ws_tril, gb_full)
        ba_p = proj[:N_PROMPT, COL_BA:COL_BA + 16]
        bat = ba_p.reshape(N_PROMPT // DN_CHUNK, DN_CHUNK, 16).transpose(0, 2, 1)
        yc_p, s_p = _delta(proj, bat, dn_conv[l], acol, dtcol, arow, dtrow, onorm)
        pp = proj[:N_PROMPT].reshape(BATCH, SEQ, PROJ_COLS)
        outs_delta_p.append(s_p)
        outs_conv_p.append(pp[:, SEQ - (CONV_W - 1):, COL_QKV:COL_QKV + CONV_DIM])
        outs_pool_p.append(pp[:, SEQ - POOL_HIST:, COL_A:COL_A + POOL_WIDTH])

        pool_t = state_pool[l].transpose(1, 0, 2)
        conv_t = state_conv[l].transpose(1, 0, 2)
        ya_s, yb_s, zv_s, npool_t, nconv_t, vec, eg_all, qk_all = _sample_rows(
            proj, pool_t, conv_t, pw, ps, gn, ws_diag, gb0, dn_conv[l], acol, dtcol)
        vec3 = vec.reshape(DEC_BATCH, 4 * H, 128)
        scal3 = jnp.concatenate([eg_all[:, H:2 * H], qk_all[:, :H]], axis=1)[:, :, None]
        z3 = proj[N_PROMPT:, COL_Z:COL_Z + 1024].reshape(DEC_BATCH, H, DN_DV)
        yc_s3, s_s = _sample_state(vec3, scal3, z3, onorm, state_delta[l])
        outs_delta_s.append(s_s)
        outs_conv_s.append(nconv_t.transpose(1, 0, 2))
        outs_pool_s.append(npool_t.transpose(1, 0, 2))
        outs_zv.append(zv_s[:, None, :])

        ya = jnp.concatenate([ya_p, ya_s], axis=0)
        yb = jnp.concatenate([yb_p, yb_s], axis=0)
        yc = jnp.concatenate([yc_p, yc_s3.reshape(DEC_BATCH, 1024)], axis=0)
        x = _merge(x, proj, b_gate[l][None], ya, yb, yc, proj_a[l].astype(BF16),
                   proj_b[l].astype(BF16), proj_c[l].astype(BF16), w_o[l].astype(BF16))

        x = _ffn(x, ffn2_norm[l][None], ffn2_wg[l].astype(BF16), ffn2_wu[l].astype(BF16),
                 ffn2_wd[l].astype(BF16))

    y = _final_norm(x, final_norm[None])
    y_prompt = y[:N_PROMPT].reshape(BATCH, SEQ, D_MODEL)
    y_sample = y[N_PROMPT:].reshape(DEC_BATCH, 1, D_MODEL)
    return (y_prompt, y_sample, jnp.stack(outs_delta_p), jnp.stack(outs_delta_s),
            jnp.stack(outs_conv_p), jnp.stack(outs_conv_s), jnp.stack(outs_pool_p),
            jnp.stack(outs_pool_s), jnp.stack(outs_zv))
```

```python
import jax
import jax.numpy as jnp
from jax import lax
from jax.experimental import pallas as pl
from jax.experimental.pallas import tpu as pltpu

F32 = jnp.float32
BF16 = jnp.bfloat16

D_MODEL = 2048
BATCH = 4
SEQ = 2048
DEPTH = 4
DEC_BATCH = 128
EPS = 1e-6
FFN_DIM = 5632
POOL_WINDOWS = (2, 4, 8, 16)
POOL_WIDTH = 512
POOL_HIST = 15
GMLP_CHUNK = 128
GMLP_WIDTH = 512
DN_HEADS = 8
DN_DK = 128
DN_DV = 128
DN_CHUNK = 64
CONV_W = 4
CONV_DIM = 3072
GROUP = 128

N_PROMPT = BATCH * SEQ
N_ROWS = N_PROMPT + DEC_BATCH

COL_QKV = 1536
COL_Z = 4608
COL_BA = 5632
COL_GATE = 5648
IN_COLS = 11792
MIX_COLS = 6144
MIX_BLOCK = 1536
Z_WIDTH = DN_HEADS * DN_DV

ROW_TILE = 1040
MERGE_TILE = 256
FFN_TILE = 256
PROJ_TILE = 1024
NORM_TILE = 128
SAMPLE_BLOCK = 8
VMEM_LIMIT = 56 * 1024 * 1024


def _rmsnorm(x, g):
    ms = jnp.mean(x * x, axis=-1, keepdims=True)
    return x * lax.rsqrt(ms + EPS) * g


def _silu(x):
    return x * jax.nn.sigmoid(x)


def _softplus(x):
    return jnp.maximum(x, 0.0) + jnp.log1p(jnp.exp(-jnp.abs(x)))


def _bdot(a, b):
    return jnp.einsum('hmk,hkn->hmn', a.astype(BF16), b.astype(BF16),
                      preferred_element_type=F32)


def _bdot_nt(a, b):
    return jnp.einsum('hmk,hnk->hmn', a.astype(BF16), b.astype(BF16),
                      preferred_element_type=F32)


def _bdot_tn(a, b):
    return jnp.einsum('hcm,hcn->hmn', a.astype(BF16), b.astype(BF16),
                      preferred_element_type=F32)


def _split(x):
    hi = x.astype(BF16)
    lo = (x - hi.astype(F32)).astype(BF16)
    return hi, lo


def _bdot3(a, b):
    ah, al = _split(a)
    bh, bl = _split(b)
    return _bdot(ah, bh) + (_bdot(ah, bl) + _bdot(al, bh))


def _ffn_kernel(x_ref, g_ref, wg_ref, wu_ref, wd_ref, o_ref, h_ref):
    @pl.when(pl.program_id(1) == 0)
    def _():
        x = x_ref[...]
        h_ref[...] = _rmsnorm(x, g_ref[...]).astype(BF16)
        o_ref[...] = x

    h = h_ref[...]
    a = jnp.dot(h, wg_ref[...].astype(BF16), preferred_element_type=F32)
    b = jnp.dot(h, wu_ref[...].astype(BF16), preferred_element_type=F32)
    act = (_silu(a) * (0.5 * b)).astype(BF16)
    o_ref[...] += jnp.dot(act, wd_ref[...].astype(BF16), preferred_element_type=F32)


def _ffn(layer, x, g, wg, wu, wd):
    rows = x.shape[0]
    return pl.pallas_call(
        _ffn_kernel,
        grid=(rows // ROW_TILE, FFN_DIM // FFN_TILE),
        in_specs=[
            pl.BlockSpec((ROW_TILE, D_MODEL), lambda i, j: (i, 0),
                         pipeline_mode=pl.Buffered(1)),
            pl.BlockSpec((1, D_MODEL), lambda i, j: (0, 0)),
            pl.BlockSpec((None, D_MODEL, FFN_TILE), lambda i, j: (layer, 0, j)),
            pl.BlockSpec((None, D_MODEL, FFN_TILE), lambda i, j: (layer, 0, j)),
            pl.BlockSpec((None, FFN_TILE, D_MODEL), lambda i, j: (layer, j, 0)),
        ],
        out_specs=pl.BlockSpec((ROW_TILE, D_MODEL), lambda i, j: (i, 0)),
        out_shape=jax.ShapeDtypeStruct((rows, D_MODEL), F32),
        scratch_shapes=[pltpu.VMEM((ROW_TILE, D_MODEL), BF16)],
        compiler_params=pltpu.CompilerParams(
            dimension_semantics=("parallel", "arbitrary"),
            vmem_limit_bytes=VMEM_LIMIT),
        name="ffn",
    )(x, g, wg, wu, wd)


def _inproj_kernel(x_ref, g_ref, w_ref, o_ref, h_ref):
    @pl.when(pl.program_id(1) == 0)
    def _():
        h_ref[...] = _rmsnorm(x_ref[...], g_ref[...]).astype(BF16)

    o_ref[...] = jnp.dot(h_ref[...], w_ref[...].astype(BF16), preferred_element_type=F32)


def _inproj(layer, x, g, w, n_cols):
    rows = x.shape[0]
    return pl.pallas_call(
        _inproj_kernel,
        grid=(rows // ROW_TILE, n_cols // PROJ_TILE),
        in_specs=[
            pl.BlockSpec((ROW_TILE, D_MODEL), lambda i, j: (i, 0)),
            pl.BlockSpec((1, D_MODEL), lambda i, j: (0, 0)),
            pl.BlockSpec((None, D_MODEL, PROJ_TILE), lambda i, j: (layer, 0, j)),
        ],
        out_specs=pl.BlockSpec((ROW_TILE, PROJ_TILE), lambda i, j: (i, j)),
        out_shape=jax.ShapeDtypeStruct((rows, n_cols), F32),
        scratch_shapes=[pltpu.VMEM((ROW_TILE, D_MODEL), BF16)],
        compiler_params=pltpu.CompilerParams(
            dimension_semantics=("parallel", "arbitrary"),
            vmem_limit_bytes=VMEM_LIMIT),
        name="inproj",
    )(x, g, w)


def _gmlp_gate(gu, gv, norm_g):
    u = jax.nn.gelu(gu)
    zv = _rmsnorm(jax.nn.gelu(gv), norm_g)
    return u, zv


def _pool_gmlp_kernel(m_ref, pw_ref, ps_ref, gn_ref, ws_ref, gb_ref,
                      ya_ref, yb_ref, ext_ref):
    c = pl.program_id(1)
    hist = 16
    rows = GMLP_CHUNK

    @pl.when(c == 0)
    def _():
        ext_ref[0:hist, :] = jnp.zeros((hist, POOL_WIDTH), F32)

    a = m_ref[:, 0:POOL_WIDTH]
    ext_ref[hist:hist + rows, :] = a
    pos = c * rows + lax.broadcasted_iota(jnp.int32, (rows, 1), 0)
    for gi, w in enumerate(POOL_WINDOWS):
        sl = slice(gi * GROUP, (gi + 1) * GROUP)
        s = a[:, sl]
        for i in range(1, w):
            s = s + ext_ref[hist - i:hist - i + rows, sl]
        cnt = jnp.minimum(pos + 1, w).astype(F32)
        d = s / cnt - a[:, sl]
        y = jnp.dot(d.astype(BF16), pw_ref[gi], preferred_element_type=F32)
        ya_ref[:, sl] = y * ps_ref[:, sl]
    ext_ref[0:hist, :] = a[rows - hist:rows, :]

    u, zv = _gmlp_gate(m_ref[:, POOL_WIDTH:POOL_WIDTH + GMLP_WIDTH],
                       m_ref[:, POOL_WIDTH + GMLP_WIDTH:MIX_BLOCK], gn_ref[...])
    for gi in range(4):
        sl = slice(gi * GROUP, (gi + 1) * GROUP)
        s = jnp.dot(ws_ref[gi], zv[:, sl].astype(BF16), preferred_element_type=F32)
        yb_ref[:, sl] = u[:, sl] * (s + gb_ref[:, sl])


def _pool_gmlp(mix, pool_w, pool_scale, gmlp_norm, ws_tril, gb_full):
    chunks = SEQ // GMLP_CHUNK
    row = lambda b, c: b * chunks + c
    full = lambda shape: pl.BlockSpec(shape, lambda b, c: (0,) * len(shape))
    return pl.pallas_call(
        _pool_gmlp_kernel,
        grid=(BATCH, chunks),
        in_specs=[
            pl.BlockSpec((GMLP_CHUNK, MIX_BLOCK), lambda b, c: (row(b, c), 0)),
            full((4, GROUP, GROUP)),
            full((1, POOL_WIDTH)),
            full((1, GMLP_WIDTH)),
            full((4, GMLP_CHUNK, GMLP_CHUNK)),
            full((GMLP_CHUNK, GMLP_WIDTH)),
        ],
        out_specs=[
            pl.BlockSpec((GMLP_CHUNK, POOL_WIDTH), lambda b, c: (row(b, c), 0)),
            pl.BlockSpec((GMLP_CHUNK, GMLP_WIDTH), lambda b, c: (row(b, c), 0)),
        ],
        out_shape=[
            jax.ShapeDtypeStruct((N_ROWS, POOL_WIDTH), F32),
            jax.ShapeDtypeStruct((N_ROWS, GMLP_WIDTH), F32),
        ],
        scratch_shapes=[pltpu.VMEM((16 + GMLP_CHUNK, POOL_WIDTH), F32)],
        compiler_params=pltpu.CompilerParams(
            dimension_semantics=("parallel", "arbitrary"),
            vmem_limit_bytes=VMEM_LIMIT),
        name="pool_gmlp",
    )(mix, pool_w, pool_scale, gmlp_norm, ws_tril, gb_full)


def _tri_inverse(m, ri, ci):
    blk = (ri // 16) == (ci // 16)
    eye = jnp.where(ri == ci, 1.0, 0.0).astype(F32)
    d = jnp.where(blk, m, 0.0)
    low = m - d
    d2 = _bdot3(d, d)
    d4 = _bdot3(d2, d2)
    d8 = _bdot3(d4, d4)
    td = eye - d
    td = td + _bdot3(td, d2)
    td = td + _bdot3(td, d4)
    td = td + _bdot3(td, d8)
    n = _bdot3(td, low)
    n2 = _bdot3(n, n)
    x = eye - n + n2 - _bdot3(n, n2)
    return _bdot3(x, td)


def _cumsum_exact(g, tri, dims):
    g1 = g.astype(BF16)
    r1 = g - g1.astype(F32)
    g2 = r1.astype(BF16)
    g3 = (r1 - g2.astype(F32)).astype(BF16)
    if dims == 'rows':
        dot = lambda t: jnp.dot(tri, t, preferred_element_type=F32)
    else:
        dot = lambda t: jnp.dot(t, tri, preferred_element_type=F32)
    return dot(g1) + (dot(g2) + dot(g3))


def _delta_kernel(qk_ref, kv_ref, zb_ref, bat_ref, cw_ref, acol_ref, dtcol_ref,
                  arow_ref, dtrow_ref, on_ref, yc_ref, s_ref, ext_ref):
    n = pl.program_id(1)
    C = DN_CHUNK
    H = DN_HEADS

    @pl.when(n == 0)
    def _():
        ext_ref[0:8, :] = jnp.zeros((8, CONV_DIM), F32)
        s_ref[...] = jnp.zeros(s_ref.shape, F32)

    ext_ref[8:8 + C, 0:MIX_BLOCK] = qk_ref[...]
    ext_ref[8:8 + C, MIX_BLOCK:CONV_DIM] = kv_ref[...]
    conv = cw_ref[3:4, :] * ext_ref[8:8 + C, :]
    for i in range(CONV_W - 1):
        conv = conv + cw_ref[i:i + 1, :] * ext_ref[5 + i:5 + i + C, :]
    ext_ref[0:8, :] = ext_ref[C:C + 8, :]
    act = _silu(conv)

    heads = lambda base: jnp.stack(
        [act[:, base + h * DN_DK:base + (h + 1) * DN_DK] for h in range(H)])
    q = heads(0)
    k = heads(H * DN_DK)
    v = heads(2 * H * DN_DK)
    q = q * lax.rsqrt(jnp.sum(q * q, axis=-1, keepdims=True) + EPS) * (DN_DK ** -0.5)
    k = k * lax.rsqrt(jnp.sum(k * k, axis=-1, keepdims=True) + EPS)

    ri = lax.broadcasted_iota(jnp.int32, (C, C), 0)
    ci = lax.broadcasted_iota(jnp.int32, (C, C), 1)
    causal = ri >= ci
    strict = ri > ci
    tril = jnp.where(causal, 1.0, 0.0).astype(BF16)
    triu = jnp.where(ri <= ci, 1.0, 0.0).astype(BF16)

    ba = zb_ref[:, Z_WIDTH:Z_WIDTH + 128]
    beta_all = jax.nn.sigmoid(ba)
    g_all = -jnp.exp(acol_ref[...]) * _softplus(ba + dtcol_ref[...])
    gc_all = _cumsum_exact(g_all, tril, 'rows')
    beta = jnp.stack([beta_all[:, h:h + 1] for h in range(H)])
    gc = jnp.stack([gc_all[:, H + h:H + h + 1] for h in range(H)])
    bat = bat_ref[0]
    g_t = -jnp.exp(arow_ref[...]) * _softplus(bat + dtrow_ref[...])
    gc_t = _cumsum_exact(g_t, triu, 'cols')
    gr = jnp.stack([gc_t[H + h:H + h + 1, :] for h in range(H)])

    diff = gc - gr
    decay = jnp.where(causal, jnp.exp(jnp.where(causal, diff, 0.0)), 0.0)
    egc = jnp.exp(gc)
    glast = gc[:, C - 1:C, :]
    kb = k * beta
    vb = v * beta
    m = jnp.where(strict, _bdot_nt(kb, k) * decay, 0.0)
    a_mat = _bdot_nt(q, k) * decay
    t = _tri_inverse(m, ri, ci)
    uw = _bdot(t, jnp.concatenate([vb, kb * egc], axis=-1))
    u = uw[:, :, :DN_DV]
    w = uw[:, :, DN_DV:]

    s = s_ref[0]
    s16 = s.astype(BF16)
    v_new = u - _bdot(w, s16)
    o = _bdot(q * egc, s16) + _bdot(a_mat, v_new)
    kg = k * jnp.exp(glast - gc)
    s_ref[0] = s * jnp.exp(glast) + _bdot_tn(kg, v_new)

    o = o * lax.rsqrt(jnp.mean(o * o, axis=-1, keepdims=True) + EPS) * on_ref[...]
    for h in range(H):
        sl = slice(h * DN_DV, (h + 1) * DN_DV)
        yc_ref[:, sl] = o[h] * _silu(zb_ref[:, sl])


def _delta(mix, bat, conv_w, acol, dtcol, arow, dtrow, onorm):
    chunks = SEQ // DN_CHUNK
    row = lambda b, n: b * chunks + n
    full = lambda shape: pl.BlockSpec(shape, lambda b, n: (0,) * len(shape))
    return pl.pallas_call(
        _delta_kernel,
        grid=(BATCH, chunks),
        in_specs=[
            pl.BlockSpec((DN_CHUNK, MIX_BLOCK), lambda b, n: (row(b, n), 1)),
            pl.BlockSpec((DN_CHUNK, MIX_BLOCK), lambda b, n: (row(b, n), 2)),
            pl.BlockSpec((DN_CHUNK, MIX_BLOCK), lambda b, n: (row(b, n), 3)),
            pl.BlockSpec((1, 16, DN_CHUNK), lambda b, n: (row(b, n), 0, 0)),
            full((CONV_W, CONV_DIM)),
            full((1, 128)),
            full((1, 128)),
            full((16, 1)),
            full((16, 1)),
            full((1, DN_DV)),
        ],
        out_specs=[
            pl.BlockSpec((DN_CHUNK, Z_WIDTH), lambda b, n: (row(b, n), 0)),
            pl.BlockSpec((1, DN_HEADS, DN_DK, DN_DV), lambda b, n: (b, 0, 0, 0)),
        ],
        out_shape=[
            jax.ShapeDtypeStruct((N_PROMPT, Z_WIDTH), F32),
            jax.ShapeDtypeStruct((BATCH, DN_HEADS, DN_DK, DN_DV), F32),
        ],
        scratch_shapes=[pltpu.VMEM((8 + DN_CHUNK, CONV_DIM), F32)],
        compiler_params=pltpu.CompilerParams(
            dimension_semantics=("parallel", "arbitrary"),
            vmem_limit_bytes=VMEM_LIMIT),
        name="delta",
    )(mix, mix, mix, bat, conv_w, acol, dtcol, arow, dtrow, onorm)


def _sample_rows_kernel(m_ref, qk_ref, kv_ref, zb_ref, pool_ref, cst_ref,
                        pw_ref, ps_ref, gn_ref, wsd_ref, gb0_ref, cw_ref, acol_ref, dtcol_ref,
                        ya_in, yb_in,
                        ya_ref, yb_ref, zv_ref, npool_ref, nconv_ref, vec_ref, eg_ref, qk_out):
    del ya_in, yb_in
    H = DN_HEADS
    a = m_ref[:, 0:POOL_WIDTH]
    for gi, w in enumerate(POOL_WINDOWS):
        sl = slice(gi * GROUP, (gi + 1) * GROUP)
        s = a[:, sl]
        for i in range(1, w):
            s = s + pool_ref[POOL_HIST - i, :, sl]
        d = s / float(w) - a[:, sl]
        y = jnp.dot(d.astype(BF16), pw_ref[gi], preferred_element_type=F32)
        ya_ref[:, sl] = y * ps_ref[:, sl]
    for i in range(POOL_HIST - 1):
        npool_ref[i] = pool_ref[i + 1]
    npool_ref[POOL_HIST - 1] = a

    u, zv = _gmlp_gate(m_ref[:, POOL_WIDTH:POOL_WIDTH + GMLP_WIDTH],
                       m_ref[:, POOL_WIDTH + GMLP_WIDTH:MIX_BLOCK], gn_ref[...])
    zv_ref[...] = zv
    yb_ref[...] = u * (wsd_ref[...] * zv + gb0_ref[...])

    x = jnp.concatenate([qk_ref[...], kv_ref[...]], axis=-1)
    conv = cw_ref[3:4, :] * x
    for i in range(CONV_W - 1):
        conv = conv + cw_ref[i:i + 1, :] * cst_ref[i]
    nconv_ref[0] = cst_ref[1]
    nconv_ref[1] = cst_ref[2]
    nconv_ref[2] = x
    act = _silu(conv)
    ba = zb_ref[:, Z_WIDTH:Z_WIDTH + 128]
    beta_all = jax.nn.sigmoid(ba)
    g_all = -jnp.exp(acol_ref[...]) * _softplus(ba + dtcol_ref[...])
    eg_all = jnp.exp(g_all)
    eg_ref[...] = eg_all
    lane = lax.broadcasted_iota(jnp.int32, (DEC_BATCH, 128), 1)
    qk_all = jnp.zeros((DEC_BATCH, 128), F32)
    for h in range(H):
        q = act[:, h * DN_DK:(h + 1) * DN_DK]
        k = act[:, (H + h) * DN_DK:(H + h + 1) * DN_DK]
        v = act[:, (2 * H + h) * DN_DK:(2 * H + h + 1) * DN_DK]
        q = q * lax.rsqrt(jnp.sum(q * q, axis=-1, keepdims=True) + EPS) * (DN_DK ** -0.5)
        k = k * lax.rsqrt(jnp.sum(k * k, axis=-1, keepdims=True) + EPS)
        beta = beta_all[:, h:h + 1]
        eg = eg_all[:, H + h:H + h + 1]
        vec_ref[:, h * 128:(h + 1) * 128] = k * (beta * eg)
        vec_ref[:, (H + h) * 128:(H + h + 1) * 128] = q * eg
        vec_ref[:, (2 * H + h) * 128:(2 * H + h + 1) * 128] = k
        vec_ref[:, (3 * H + h) * 128:(3 * H + h + 1) * 128] = v * beta
        qk = jnp.sum(q * k, axis=-1, keepdims=True)
        qk_all = jnp.where(lane == h, qk, qk_all)
    qk_out[...] = qk_all


def _sample_rows(mix, pool_t, conv_t, pool_w, pool_scale, gmlp_norm, ws_diag, gb0, conv_w,
                 acol, dtcol, ya, yb):
    rb = N_PROMPT // DEC_BATCH
    full = lambda shape: pl.BlockSpec(shape, lambda i: (0,) * len(shape))
    n = DEC_BATCH
    hbm = pl.BlockSpec(memory_space=pl.ANY)
    return pl.pallas_call(
        _sample_rows_kernel,
        grid=(1,),
        in_specs=[
            pl.BlockSpec((n, MIX_BLOCK), lambda i: (rb, 0)),
            pl.BlockSpec((n, MIX_BLOCK), lambda i: (rb, 1)),
            pl.BlockSpec((n, MIX_BLOCK), lambda i: (rb, 2)),
            pl.BlockSpec((n, MIX_BLOCK), lambda i: (rb, 3)),
            full((POOL_HIST, n, POOL_WIDTH)),
            full((CONV_W - 1, n, CONV_DIM)),
            full((4, GROUP, GROUP)),
            full((1, POOL_WIDTH)),
            full((1, GMLP_WIDTH)),
            full((1, GMLP_WIDTH)),
            full((1, GMLP_WIDTH)),
            full((CONV_W, CONV_DIM)),
            full((1, 128)),
            full((1, 128)),
            hbm,
            hbm,
        ],
        out_specs=[
            pl.BlockSpec((n, POOL_WIDTH), lambda i: (rb, 0)),
            pl.BlockSpec((n, GMLP_WIDTH), lambda i: (rb, 0)),
            full((n, GMLP_WIDTH)),
            full((POOL_HIST, n, POOL_WIDTH)),
            full((CONV_W - 1, n, CONV_DIM)),
            full((n, 4 * DN_HEADS * 128)),
            full((n, 128)),
            full((n, 128)),
        ],
        out_shape=[
            jax.ShapeDtypeStruct((N_ROWS, POOL_WIDTH), F32),
            jax.ShapeDtypeStruct((N_ROWS, GMLP_WIDTH), F32),
            jax.ShapeDtypeStruct((n, GMLP_WIDTH), F32),
            jax.ShapeDtypeStruct((POOL_HIST, n, POOL_WIDTH), F32),
            jax.ShapeDtypeStruct((CONV_W - 1, n, CONV_DIM), F32),
            jax.ShapeDtypeStruct((n, 4 * DN_HEADS * 128), F32),
            jax.ShapeDtypeStruct((n, 128), F32),
            jax.ShapeDtypeStruct((n, 128), F32),
        ],
        input_output_aliases={14: 0, 15: 1},
        compiler_params=pltpu.CompilerParams(
            dimension_semantics=("arbitrary",),
            vmem_limit_bytes=VMEM_LIMIT),
        name="sample_rows",
    )(mix, mix, mix, mix, pool_t, conv_t, pool_w, pool_scale, gmlp_norm,
      ws_diag, gb0, conv_w, acol, dtcol, ya, yb)


def _sample_state_kernel(vec_ref, scal_ref, z_ref, on_ref, s_ref, *rest):
    yc_ref, so_ref = rest[-2:]
    H = DN_HEADS
    row = lax.broadcasted_iota(jnp.int32, (1, 8, 1), 1)
    for h in range(H):
        s = s_ref[0, :, h]
        w = vec_ref[:, h:h + 1, :]
        qg = vec_ref[:, H + h:H + h + 1, :]
        k = vec_ref[:, 2 * H + h:2 * H + h + 1, :]
        vb = vec_ref[:, 3 * H + h:3 * H + h + 1, :]
        eg = scal_ref[:, h:h + 1, :]
        qk = scal_ref[:, H + h:H + h + 1, :]
        lhs = jnp.where(row == 0, w, jnp.where(row == 1, qg, 0.0))
        r = _bdot(lhs, s)
        v_new = vb - r[:, 0:1, :]
        o = r[:, 1:2, :] + qk * v_new
        k8 = jnp.where(row == 0, k, 0.0)
        v8 = jnp.broadcast_to(v_new, k8.shape)
        so_ref[0, :, h] = s * eg + _bdot_tn(k8, v8)
        o = o * lax.rsqrt(jnp.mean(o * o, axis=-1, keepdims=True) + EPS) * on_ref[...]
        yc_ref[:, h:h + 1, :] = o * _silu(z_ref[:, h:h + 1, :])


def _sample_state(layer, vec3, scal3, z3, onorm, state, stacked):
    sb = SAMPLE_BLOCK
    H = DN_HEADS
    in_specs = [
        pl.BlockSpec((sb, 4 * H, 128), lambda i: (i, 0, 0)),
        pl.BlockSpec((sb, 2 * H, 1), lambda i: (i, 0, 0)),
        pl.BlockSpec((sb, H, DN_DV), lambda i: (i, 0, 0)),
        pl.BlockSpec((1, DN_DV), lambda i: (0, 0)),
        pl.BlockSpec((1, sb, H, DN_DK, DN_DV), lambda i: (layer, i, 0, 0, 0)),
    ]
    args = [vec3, scal3, z3, onorm, state]
    aliases = {}
    if stacked is not None:
        in_specs.append(pl.BlockSpec(memory_space=pl.ANY))
        args.append(stacked)
        aliases = {5: 1}
    return pl.pallas_call(
        _sample_state_kernel,
        grid=(DEC_BATCH // sb,),
        in_specs=in_specs,
        out_specs=[
            pl.BlockSpec((sb, H, DN_DV), lambda i: (i, 0, 0)),
            pl.BlockSpec((1, sb, H, DN_DK, DN_DV), lambda i: (layer, i, 0, 0, 0)),
        ],
        out_shape=[
            jax.ShapeDtypeStruct((DEC_BATCH, H, DN_DV), F32),
            jax.ShapeDtypeStruct((DEPTH, DEC_BATCH, H, DN_DK, DN_DV), F32),
        ],
        input_output_aliases=aliases,
        compiler_params=pltpu.CompilerParams(
            dimension_semantics=("parallel",),
            vmem_limit_bytes=VMEM_LIMIT),
        name="sample_state",
    )(*args)


def _merge_kernel(x_ref, g0_ref, g1_ref, g2_ref, bg_ref, ya_ref, yb_ref, yc_ref,
                  pa_ref, pb_ref, pc_ref, wo_ref, *rest):
    o_ref = rest[-1]

    def branch(g_ref, i, y_ref, p_ref):
        gate = jax.nn.sigmoid(g_ref[...] + bg_ref[:, i * D_MODEL:(i + 1) * D_MODEL])
        return gate * jnp.dot(y_ref[...].astype(BF16), p_ref[...], preferred_element_type=F32)

    m = branch(g0_ref, 0, ya_ref, pa_ref)
    m = m + branch(g1_ref, 1, yb_ref, pb_ref)
    m = m + branch(g2_ref, 2, yc_ref, pc_ref)
    o_ref[...] = x_ref[...] + jnp.dot(m.astype(BF16), wo_ref[...], preferred_element_type=F32)


def _merge(x, gates, b_gate, ya, yb, yc, pa, pb, pc, wo, tile, first, count, prev):
    once = lambda shape: pl.BlockSpec(shape, lambda i: (0,) * len(shape),
                                      pipeline_mode=pl.Buffered(1))
    rows = lambda width, col=0: pl.BlockSpec((tile, width), lambda i: (first + i, col))
    in_specs = [
        rows(D_MODEL),
        rows(D_MODEL, 0), rows(D_MODEL, 1), rows(D_MODEL, 2),
        once((1, 3 * D_MODEL)),
        rows(POOL_WIDTH), rows(GMLP_WIDTH),
        pl.BlockSpec((tile, Z_WIDTH), lambda i: (i, 0)),
        once((POOL_WIDTH, D_MODEL)),
        once((GMLP_WIDTH, D_MODEL)),
        once((Z_WIDTH, D_MODEL)),
        once((D_MODEL, D_MODEL)),
    ]
    args = [x, gates, gates, gates, b_gate, ya, yb, yc, pa, pb, pc, wo]
    aliases = {}
    if prev is not None:
        in_specs.append(pl.BlockSpec(memory_space=pl.ANY))
        args.append(prev)
        aliases = {12: 0}
    return pl.pallas_call(
        _merge_kernel,
        grid=(count,),
        in_specs=in_specs,
        out_specs=rows(D_MODEL),
        out_shape=jax.ShapeDtypeStruct((N_ROWS, D_MODEL), F32),
        input_output_aliases=aliases,
        compiler_params=pltpu.CompilerParams(
            dimension_semantics=("parallel",),
            vmem_limit_bytes=VMEM_LIMIT),
        name="merge",
    )(*args)


def _final_kernel(x_ref, g_ref, o_ref):
    o_ref[...] = _rmsnorm(x_ref[...], g_ref[...])


def _final_norm(x, g, first, count):
    t = NORM_TILE
    return pl.pallas_call(
        _final_kernel,
        grid=(count,),
        in_specs=[pl.BlockSpec((t, D_MODEL), lambda i: (first + i, 0)),
                  pl.BlockSpec((1, D_MODEL), lambda i: (0, 0))],
        out_specs=pl.BlockSpec((t, D_MODEL), lambda i: (i, 0)),
        out_shape=jax.ShapeDtypeStruct((count * t, D_MODEL), F32),
        compiler_params=pltpu.CompilerParams(dimension_semantics=("parallel",),
                                             vmem_limit_bytes=VMEM_LIMIT),
        name="final_norm",
    )(x, g)


def _lane_pad(v, offset, width=128):
    return jnp.zeros((1, width), F32).at[0, offset:offset + v.shape[0]].set(v)


def _tail_rows(mix, count, col, width):
    return jnp.stack([
        lax.slice(mix, ((b + 1) * SEQ - count, col), ((b + 1) * SEQ, col + width))
        for b in range(BATCH)])


def kernel(x_prompt, x_sample, state_delta, state_conv, state_pool, ffn1_norm, ffn1_wg, ffn1_wu, ffn1_wd, mix_norm, w_in, b_gate, pool_w, pool_scale, gmlp_norm, gmlp_ws, gmlp_b, dn_conv, dn_a_log, dn_dt_bias, dn_onorm, proj_a, proj_b, proj_c, w_o, ffn2_norm, ffn2_wg, ffn2_wu, ffn2_wd, final_norm):
    H = DN_HEADS
    x = jnp.concatenate([x_prompt.reshape(N_PROMPT, D_MODEL),
                         x_sample.reshape(DEC_BATCH, D_MODEL)], axis=0)
    causal = jnp.tril(jnp.ones((GMLP_CHUNK, GMLP_CHUNK), bool))
    outs_delta_p, outs_conv_p, outs_conv_s = [], [], []
    outs_pool_p, outs_pool_s, outs_zv = [], [], []
    delta_s = None
    w_gate = w_in[:, :, COL_GATE:]
    for l in range(DEPTH):
        x = _ffn(l, x, ffn1_norm[l][None], ffn1_wg, ffn1_wu, ffn1_wd)

        mix = _inproj(l, x, mix_norm[l][None], w_in, MIX_COLS)
        gates = _inproj(l, x, mix_norm[l][None], w_gate, 3 * D_MODEL)

        pw = pool_w[l].astype(BF16)
        ps = pool_scale[l][None]
        gn = gmlp_norm[l][None]
        ws_tril = jnp.where(causal, gmlp_ws[l], 0).astype(BF16)
        gb_full = jnp.repeat(gmlp_b[l].T, GROUP, axis=1)
        ws_diag = jnp.repeat(gmlp_ws[l][:, 0, 0], GROUP)[None]
        gb0 = jnp.repeat(gmlp_b[l][:, 0], GROUP)[None]
        acol = _lane_pad(dn_a_log[l], H)
        dtcol = _lane_pad(dn_dt_bias[l], H)
        arow = acol[0, :16][:, None]
        dtrow = dtcol[0, :16][:, None]
        onorm = dn_onorm[l][None]

        ya, yb = _pool_gmlp(mix, pw, ps, gn, ws_tril, gb_full)
        ba_p = lax.slice(mix, (0, COL_BA), (N_PROMPT, COL_BA + 16))
        bat = ba_p.reshape(N_PROMPT // DN_CHUNK, DN_CHUNK, 16).transpose(0, 2, 1)
        yc_p, s_p = _delta(mix, bat, dn_conv[l], acol, dtcol, arow, dtrow, onorm)
        outs_delta_p.append(s_p)
        outs_conv_p.append(_tail_rows(mix, CONV_W - 1, COL_QKV, CONV_DIM))
        outs_pool_p.append(_tail_rows(mix, POOL_HIST, 0, POOL_WIDTH))

        pool_t = state_pool[l].transpose(1, 0, 2)
        conv_t = state_conv[l].transpose(1, 0, 2)
        ya, yb, zv_s, npool_t, nconv_t, vec, eg_all, qk_all = _sample_rows(
            mix, pool_t, conv_t, pw, ps, gn, ws_diag, gb0, dn_conv[l], acol, dtcol, ya, yb)
        vec3 = vec.reshape(DEC_BATCH, 4 * H, 128)
        scal3 = jnp.concatenate([eg_all[:, H:2 * H], qk_all[:, :H]], axis=1)[:, :, None]
        z3 = lax.slice(mix, (N_PROMPT, COL_Z), (N_ROWS, COL_Z + Z_WIDTH)).reshape(DEC_BATCH, H, DN_DV)
        yc_s3, delta_s = _sample_state(l, vec3, scal3, z3, onorm, state_delta, delta_s)
        outs_conv_s.append(nconv_t.transpose(1, 0, 2))
        outs_pool_s.append(npool_t.transpose(1, 0, 2))
        outs_zv.append(zv_s[:, None, :])

        weights = (proj_a[l].astype(BF16), proj_b[l].astype(BF16), proj_c[l].astype(BF16),
                   w_o[l].astype(BF16))
        bg = b_gate[l][None]
        x_new = _merge(x, gates, bg, ya, yb, yc_p, *weights,
                       tile=MERGE_TILE, first=0, count=N_PROMPT // MERGE_TILE, prev=None)
        x = _merge(x, gates, bg, ya, yb, yc_s3.reshape(DEC_BATCH, Z_WIDTH), *weights,
                   tile=DEC_BATCH, first=N_PROMPT // DEC_BATCH, count=1, prev=x_new)

        x = _ffn(l, x, ffn2_norm[l][None], ffn2_wg, ffn2_wu, ffn2_wd)

    g = final_norm[None]
    y_prompt = _final_norm(x, g, 0, N_PROMPT // NORM_TILE).reshape(BATCH, SEQ, D_MODEL)
    y_sample = _final_norm(x, g, N_PROMPT // NORM_TILE, DEC_BATCH // NORM_TILE).reshape(
        DEC_BATCH, 1, D_MODEL)
    return (y_prompt, y_sample, jnp.stack(outs_delta_p), delta_s,
            jnp.stack(outs_conv_p), jnp.stack(outs_conv_s), jnp.stack(outs_pool_p),
            jnp.stack(outs_pool_s), jnp.stack(outs_zv))
```

```python
import jax
import jax.numpy as jnp
from jax import lax
from jax.experimental import pallas as pl
from jax.experimental.pallas import tpu as pltpu

F32 = jnp.float32
BF16 = jnp.bfloat16

D_MODEL = 2048
BATCH = 4
SEQ = 2048
DEPTH = 4
DEC_BATCH = 128
EPS = 1e-6
FFN_DIM = 5632
POOL_WINDOWS = (2, 4, 8, 16)
POOL_WIDTH = 512
POOL_HIST = 15
GMLP_CHUNK = 128
GMLP_WIDTH = 512
DN_HEADS = 8
DN_DK = 128
DN_DV = 128
DN_CHUNK = 64
CONV_W = 4
CONV_DIM = 3072
GROUP = 128

N_PROMPT = BATCH * SEQ
N_ROWS = N_PROMPT + DEC_BATCH

COL_QKV = 1536
COL_Z = 4608
COL_BA = 5632
COL_GATE = 5648
IN_COLS = 11792
PROJ_COLS = 12288
MIX_BLOCK = 1536
Z_WIDTH = DN_HEADS * DN_DV
GATE_SHIFT = COL_GATE - COL_BA
GATE_WIN = D_MODEL + 128

ROW_TILE = 1040
MERGE_TILE = 256
FFN_TILE = 256
PROJ_TILE = 1024
NORM_TILE = 512
DELTA_ROWS = 128
SAMPLE_BLOCK = 8
VMEM_LIMIT = 56 * 1024 * 1024
FFN_VMEM_LIMIT = 60 * 1024 * 1024


def _rmsnorm(x, g):
    ms = jnp.mean(x * x, axis=-1, keepdims=True)
    return x * lax.rsqrt(ms + EPS) * g


def _silu(x):
    return x * jax.nn.sigmoid(x)


def _softplus(x):
    return jnp.maximum(x, 0.0) + jnp.log1p(jnp.exp(-jnp.abs(x)))


def _bdot(a, b):
    return jnp.einsum('hmk,hkn->hmn', a.astype(BF16), b.astype(BF16),
                      preferred_element_type=F32)


def _bdot_nt(a, b):
    return jnp.einsum('hmk,hnk->hmn', a.astype(BF16), b.astype(BF16),
                      preferred_element_type=F32)


def _bdot_tn(a, b):
    return jnp.einsum('hcm,hcn->hmn', a.astype(BF16), b.astype(BF16),
                      preferred_element_type=F32)


def _split(x):
    hi = x.astype(BF16)
    lo = (x - hi.astype(F32)).astype(BF16)
    return hi, lo


def _bdot3(a, b):
    ah, al = _split(a)
    bh, bl = _split(b)
    return _bdot(ah, bh) + (_bdot(ah, bl) + _bdot(al, bh))


def _ffn_kernel(x_ref, g_ref, wg_ref, wu_ref, wd_ref, o_ref, h_ref):
    @pl.when(pl.program_id(1) == 0)
    def _():
        x = x_ref[...]
        h_ref[...] = _rmsnorm(x, g_ref[...]).astype(BF16)
        o_ref[...] = x

    h = h_ref[...]
    a = jnp.dot(h, wg_ref[...].astype(BF16), preferred_element_type=F32)
    b = jnp.dot(h, wu_ref[...].astype(BF16), preferred_element_type=F32)
    act = (_silu(a) * (0.5 * b)).astype(BF16)
    o_ref[...] += jnp.dot(act, wd_ref[...].astype(BF16), preferred_element_type=F32)


def _ffn(layer, x, g, wg, wu, wd, tm=ROW_TILE, tf=FFN_TILE):
    rows = x.shape[0]
    return pl.pallas_call(
        _ffn_kernel,
        grid=(rows // tm, FFN_DIM // tf),
        in_specs=[
            pl.BlockSpec((tm, D_MODEL), lambda i, j: (i, 0),
                         pipeline_mode=pl.Buffered(1)),
            pl.BlockSpec((1, D_MODEL), lambda i, j: (0, 0)),
            pl.BlockSpec((None, D_MODEL, tf), lambda i, j: (layer, 0, j)),
            pl.BlockSpec((None, D_MODEL, tf), lambda i, j: (layer, 0, j)),
            pl.BlockSpec((None, tf, D_MODEL), lambda i, j: (layer, j, 0)),
        ],
        out_specs=pl.BlockSpec((tm, D_MODEL), lambda i, j: (i, 0)),
        out_shape=jax.ShapeDtypeStruct((rows, D_MODEL), F32),
        scratch_shapes=[pltpu.VMEM((tm, D_MODEL), BF16)],
        compiler_params=pltpu.CompilerParams(
            dimension_semantics=("parallel", "arbitrary"),
            vmem_limit_bytes=FFN_VMEM_LIMIT),
        name="ffn",
    )(x, g, wg, wu, wd)


def _inproj_kernel(x_ref, g_ref, w_ref, o_ref, h_ref):
    j = pl.program_id(1)

    @pl.when(j == 0)
    def _():
        h_ref[...] = _rmsnorm(x_ref[...], g_ref[...]).astype(BF16)

    col = j * PROJ_TILE + lax.broadcasted_iota(jnp.int32, (1, PROJ_TILE), 1)
    w = jnp.where(col < IN_COLS, w_ref[...], 0.0).astype(BF16)
    o_ref[...] = jnp.dot(h_ref[...], w, preferred_element_type=F32)


def _inproj(layer, x, g, w):
    rows = x.shape[0]
    n_cols = PROJ_COLS
    return pl.pallas_call(
        _inproj_kernel,
        grid=(rows // ROW_TILE, n_cols // PROJ_TILE),
        in_specs=[
            pl.BlockSpec((ROW_TILE, D_MODEL), lambda i, j: (i, 0)),
            pl.BlockSpec((1, D_MODEL), lambda i, j: (0, 0)),
            pl.BlockSpec((None, D_MODEL, PROJ_TILE), lambda i, j: (layer, 0, j)),
        ],
        out_specs=pl.BlockSpec((ROW_TILE, PROJ_TILE), lambda i, j: (i, j)),
        out_shape=jax.ShapeDtypeStruct((rows, n_cols), F32),
        scratch_shapes=[pltpu.VMEM((ROW_TILE, D_MODEL), BF16)],
        compiler_params=pltpu.CompilerParams(
            dimension_semantics=("parallel", "arbitrary"),
            vmem_limit_bytes=VMEM_LIMIT),
        name="inproj",
    )(x, g, w)


def _gmlp_gate(gu, gv, norm_g):
    u = jax.nn.gelu(gu)
    zv = _rmsnorm(jax.nn.gelu(gv), norm_g)
    return u, zv


def _pool_gmlp_kernel(m_ref, pw_ref, ps_ref, gn_ref, ws_ref, gb_ref,
                      ya_ref, yb_ref, ext_ref):
    c = pl.program_id(1)
    hist = 16
    rows = GMLP_CHUNK

    @pl.when(c == 0)
    def _():
        ext_ref[0:hist, :] = jnp.zeros((hist, POOL_WIDTH), F32)

    a = m_ref[:, 0:POOL_WIDTH]
    ext_ref[hist:hist + rows, :] = a
    pos = c * rows + lax.broadcasted_iota(jnp.int32, (rows, 1), 0)
    for gi, w in enumerate(POOL_WINDOWS):
        sl = slice(gi * GROUP, (gi + 1) * GROUP)
        s = a[:, sl]
        for i in range(1, w):
            s = s + ext_ref[hist - i:hist - i + rows, sl]
        cnt = jnp.minimum(pos + 1, w).astype(F32)
        d = s / cnt - a[:, sl]
        y = jnp.dot(d.astype(BF16), pw_ref[gi], preferred_element_type=F32)
        ya_ref[:, sl] = y * ps_ref[:, sl]
    ext_ref[0:hist, :] = a[rows - hist:rows, :]

    u, zv = _gmlp_gate(m_ref[:, POOL_WIDTH:POOL_WIDTH + GMLP_WIDTH],
                       m_ref[:, POOL_WIDTH + GMLP_WIDTH:MIX_BLOCK], gn_ref[...])
    for gi in range(4):
        sl = slice(gi * GROUP, (gi + 1) * GROUP)
        s = jnp.dot(ws_ref[gi], zv[:, sl].astype(BF16), preferred_element_type=F32)
        yb_ref[:, sl] = u[:, sl] * (s + gb_ref[:, sl])


def _pool_gmlp(mix, pool_w, pool_scale, gmlp_norm, ws_tril, gb_full):
    chunks = SEQ // GMLP_CHUNK
    row = lambda b, c: b * chunks + c
    full = lambda shape: pl.BlockSpec(shape, lambda b, c: (0,) * len(shape))
    return pl.pallas_call(
        _pool_gmlp_kernel,
        grid=(BATCH, chunks),
        in_specs=[
            pl.BlockSpec((GMLP_CHUNK, MIX_BLOCK), lambda b, c: (row(b, c), 0)),
            full((4, GROUP, GROUP)),
            full((1, POOL_WIDTH)),
            full((1, GMLP_WIDTH)),
            full((4, GMLP_CHUNK, GMLP_CHUNK)),
            full((GMLP_CHUNK, GMLP_WIDTH)),
        ],
        out_specs=[
            pl.BlockSpec((GMLP_CHUNK, POOL_WIDTH), lambda b, c: (row(b, c), 0)),
            pl.BlockSpec((GMLP_CHUNK, GMLP_WIDTH), lambda b, c: (row(b, c), 0)),
        ],
        out_shape=[
            jax.ShapeDtypeStruct((N_ROWS, POOL_WIDTH), F32),
            jax.ShapeDtypeStruct((N_ROWS, GMLP_WIDTH), F32),
        ],
        scratch_shapes=[pltpu.VMEM((16 + GMLP_CHUNK, POOL_WIDTH), F32)],
        compiler_params=pltpu.CompilerParams(
            dimension_semantics=("parallel", "arbitrary"),
            vmem_limit_bytes=VMEM_LIMIT),
        name="pool_gmlp",
    )(mix, pool_w, pool_scale, gmlp_norm, ws_tril, gb_full)


def _pair_bd(y, left):
    zero = jnp.zeros_like(y)
    return jnp.concatenate([jnp.where(left, y, zero), jnp.where(left, zero, y)],
                           axis=1).astype(BF16)


def _pair_mm(x, y_bd):
    return _bdot(x, y_bd)


def _pair_inverse(m, left, blk, eye):
    d = jnp.where(blk, m, 0.0)
    low = m - d
    d2 = _pair_mm(d, _pair_bd(d, left))
    d2_bd = _pair_bd(d2, left)
    d4 = _pair_mm(d2, d2_bd)
    d4_bd = _pair_bd(d4, left)
    d8_bd = _pair_bd(_pair_mm(d4, d4_bd), left)
    td = eye - d
    td = td + _pair_mm(td, d2_bd)
    td = td + _pair_mm(td, d4_bd)
    td = td + _pair_mm(td, d8_bd)
    n = _pair_mm(td, _pair_bd(low, left))
    n2 = _pair_mm(n, _pair_bd(n, left))
    x = eye - n + n2 - _pair_mm(n, _pair_bd(n2, left))
    t0 = _pair_mm(x, _pair_bd(td, left))
    th, tl = _split(t0)
    mh, ml = _split(m)
    th_bd = _pair_bd(th, left)
    tl_bd = _pair_bd(tl, left)
    mt = _bdot(mh, th_bd) + (_bdot(mh, tl_bd) + _bdot(ml, th_bd))
    r = eye - t0 - mt
    return t0 + _bdot(th, _pair_bd(r, left))


def _cumsum_exact(g, tri, dims):
    g1 = g.astype(BF16)
    r1 = g - g1.astype(F32)
    g2 = r1.astype(BF16)
    g3 = (r1 - g2.astype(F32)).astype(BF16)
    if dims == 'rows':
        dot = lambda t: jnp.dot(tri, t, preferred_element_type=F32)
    else:
        dot = lambda t: jnp.dot(t, tri, preferred_element_type=F32)
    return dot(g1) + (dot(g2) + dot(g3))


def _delta_kernel(qk_ref, kv_ref, zb_ref, bat_ref, cw_ref, acol_ref, dtcol_ref,
                  arow_ref, dtrow_ref, on_ref, yc_ref, s_ref, ext_ref):
    n = pl.program_id(1)
    C = DN_CHUNK
    H = DN_HEADS
    R = DELTA_ROWS

    @pl.when(n == 0)
    def _():
        ext_ref[0:8, :] = jnp.zeros((8, CONV_DIM), F32)
        s_ref[...] = jnp.zeros(s_ref.shape, F32)

    ext_ref[8:8 + R, 0:MIX_BLOCK] = qk_ref[...]
    ext_ref[8:8 + R, MIX_BLOCK:CONV_DIM] = kv_ref[...]
    conv = cw_ref[3:4, :] * ext_ref[8:8 + R, :]
    for i in range(CONV_W - 1):
        conv = conv + cw_ref[i:i + 1, :] * ext_ref[5 + i:5 + i + R, :]
    ext_ref[0:8, :] = ext_ref[R:R + 8, :]
    act = _silu(conv)

    ri = lax.broadcasted_iota(jnp.int32, (C, 2 * C), 0)
    li = lax.broadcasted_iota(jnp.int32, (C, 2 * C), 1)
    ci = li % C
    left = li < C
    causal = ri >= ci
    strict = ri > ci
    blk = (ri // 16) == (ci // 16)
    eye = jnp.where(ri == ci, 1.0, 0.0).astype(F32)
    zero = jnp.zeros((C, DN_DK), F32)

    rr = lax.broadcasted_iota(jnp.int32, (R, R), 0)
    cc = lax.broadcasted_iota(jnp.int32, (R, R), 1)
    tril_chunks = jnp.where((rr // C == cc // C) & (rr >= cc), 1.0, 0.0).astype(BF16)
    pr = lax.broadcasted_iota(jnp.int32, (2 * C, 2 * C), 0)
    pc = lax.broadcasted_iota(jnp.int32, (2 * C, 2 * C), 1)
    triu_pair = jnp.where((pr // C == pc // C) & (pr <= pc), 1.0, 0.0).astype(BF16)

    ba = zb_ref[:, Z_WIDTH:Z_WIDTH + 128]
    beta_all = jax.nn.sigmoid(ba)
    g_all = -jnp.exp(acol_ref[...]) * _softplus(ba + dtcol_ref[...])
    gc_all = _cumsum_exact(g_all, tril_chunks, 'rows')
    on = on_ref[...]

    NC = R // C
    NP = H // 2
    lane2 = lambda a, b: jnp.concatenate([a, b], axis=1)
    diag2 = lambda a, b: jnp.concatenate([lane2(a, jnp.zeros_like(b)),
                                          lane2(jnp.zeros_like(a), b)], axis=0)

    ks, kbs, qes, gcs = [], [], [], []
    kq_lhs, kq_rhs, gc_pair, gr_pair, uw_rhs = [], [], [], [], []
    for c in range(NC):
        rows = slice(c * C, (c + 1) * C)
        heads = lambda base: jnp.stack(
            [act[rows, (base + h) * DN_DK:(base + h + 1) * DN_DK] for h in range(H)])
        q = heads(0)
        k = heads(H)
        v = heads(2 * H)
        q = q * lax.rsqrt(jnp.sum(q * q, axis=-1, keepdims=True) + EPS) * (DN_DK ** -0.5)
        k = k * lax.rsqrt(jnp.sum(k * k, axis=-1, keepdims=True) + EPS)
        beta = jnp.stack([beta_all[rows, h:h + 1] for h in range(H)])
        gc = jnp.stack([gc_all[rows, H + h:H + h + 1] for h in range(H)])
        egc = jnp.exp(gc)
        kb = k * beta
        vb = v * beta
        kbe = kb * egc
        ks.append(k)
        kbs.append(kb)
        qes.append(q * egc)
        gcs.append(gc)
        g_t = -jnp.exp(arow_ref[...]) * _softplus(bat_ref[c] + dtrow_ref[...])
        gr_all = _cumsum_exact(g_t, triu_pair, 'cols')
        for p in range(NP):
            a, b = 2 * p, 2 * p + 1
            kq_lhs.append(jnp.concatenate([lane2(kb[a], kb[b]), lane2(q[a], q[b])], axis=0))
            kq_rhs.append(diag2(k[a], k[b]))
            gc_pair.append(jnp.where(left, gc[a], gc[b]))
            gr_pair.append(gr_all[p:p + 1, :])
            uw_rhs.append(diag2(lane2(vb[a], kbe[a]), lane2(vb[b], kbe[b])))

    kq = jnp.einsum('bmk,bnk->bmn', jnp.stack(kq_lhs).astype(BF16), jnp.stack(kq_rhs).astype(BF16),
                    preferred_element_type=F32)
    diff = jnp.stack(gc_pair) - jnp.stack(gr_pair)
    decay = jnp.where(causal, jnp.exp(jnp.where(causal, diff, 0.0)), 0.0)
    m = jnp.where(strict, kq[:, 0:C] * decay, 0.0)
    a_mat = kq[:, C:2 * C] * decay
    t = _pair_inverse(m, left, blk, eye)
    uw = _bdot(t, jnp.stack(uw_rhs))

    for c in range(NC):
        rows = slice(c * C, (c + 1) * C)
        part = lambda h, i: uw[c * NP + h // 2][:, (2 * (h % 2) + i) * DN_DV:
                                               (2 * (h % 2) + i + 1) * DN_DV]
        u = jnp.stack([part(h, 0) for h in range(H)])
        w = jnp.stack([part(h, 1) for h in range(H)])
        s = s_ref[0]
        ws_qs = _bdot(jnp.concatenate([w, qes[c]], axis=1), s)
        v_new = u - ws_qs[:, 0:C]
        glast = gcs[c][:, C - 1:C, :]
        kg = ks[c] * jnp.exp(glast - gcs[c])
        s_ref[0] = s * jnp.exp(glast) + _bdot_tn(kg, v_new)
        vn_bd = jnp.stack([diag2(v_new[2 * p], v_new[2 * p + 1]) for p in range(NP)])
        av = _bdot(a_mat[c * NP:(c + 1) * NP], vn_bd)
        o = ws_qs[:, C:2 * C] + jnp.stack(
            [av[h // 2][:, (h % 2) * DN_DV:(h % 2 + 1) * DN_DV] for h in range(H)])
        o = o * lax.rsqrt(jnp.mean(o * o, axis=-1, keepdims=True) + EPS) * on
        for h in range(H):
            sl = slice(h * DN_DV, (h + 1) * DN_DV)
            yc_ref[rows, sl] = o[h] * _silu(zb_ref[rows, sl])


def _delta(mix, bat, conv_w, acol, dtcol, arow, dtrow, onorm):
    steps = SEQ // DELTA_ROWS
    cps = DELTA_ROWS // DN_CHUNK
    row = lambda b, n: b * steps + n
    full = lambda shape: pl.BlockSpec(shape, lambda b, n: (0,) * len(shape))
    return pl.pallas_call(
        _delta_kernel,
        grid=(BATCH, steps),
        in_specs=[
            pl.BlockSpec((DELTA_ROWS, MIX_BLOCK), lambda b, n: (row(b, n), 1)),
            pl.BlockSpec((DELTA_ROWS, MIX_BLOCK), lambda b, n: (row(b, n), 2)),
            pl.BlockSpec((DELTA_ROWS, MIX_BLOCK), lambda b, n: (row(b, n), 3)),
            pl.BlockSpec((cps, 8, 2 * DN_CHUNK), lambda b, n: (row(b, n), 0, 0)),
            full((CONV_W, CONV_DIM)),
            full((1, 128)),
            full((1, 128)),
            full((8, 2 * DN_CHUNK)),
            full((8, 2 * DN_CHUNK)),
            full((1, DN_DV)),
        ],
        out_specs=[
            pl.BlockSpec((DELTA_ROWS, Z_WIDTH), lambda b, n: (row(b, n), 0)),
            pl.BlockSpec((1, DN_HEADS, DN_DK, DN_DV), lambda b, n: (b, 0, 0, 0)),
        ],
        out_shape=[
            jax.ShapeDtypeStruct((N_PROMPT, Z_WIDTH), F32),
            jax.ShapeDtypeStruct((BATCH, DN_HEADS, DN_DK, DN_DV), F32),
        ],
        scratch_shapes=[pltpu.VMEM((8 + DELTA_ROWS, CONV_DIM), F32)],
        compiler_params=pltpu.CompilerParams(
            dimension_semantics=("parallel", "arbitrary"),
            vmem_limit_bytes=VMEM_LIMIT),
        name="delta",
    )(mix, mix, mix, bat, conv_w, acol, dtcol, arow, dtrow, onorm)


def _sample_rows_kernel(m_ref, qk_ref, kv_ref, zb_ref, pool_ref, cst_ref,
                        pw_ref, ps_ref, gn_ref, wsd_ref, gb0_ref, cw_ref, acol_ref, dtcol_ref,
                        ya_in, yb_in,
                        ya_ref, yb_ref, zv_ref, npool_ref, nconv_ref, vec_ref, eg_ref, qk_out):
    del ya_in, yb_in
    H = DN_HEADS
    a = m_ref[:, 0:POOL_WIDTH]
    for gi, w in enumerate(POOL_WINDOWS):
        sl = slice(gi * GROUP, (gi + 1) * GROUP)
        s = a[:, sl]
        for i in range(1, w):
            s = s + pool_ref[POOL_HIST - i, :, sl]
        d = s / float(w) - a[:, sl]
        y = jnp.dot(d.astype(BF16), pw_ref[gi], preferred_element_type=F32)
        ya_ref[:, sl] = y * ps_ref[:, sl]
    for i in range(POOL_HIST - 1):
        npool_ref[i] = pool_ref[i + 1]
    npool_ref[POOL_HIST - 1] = a

    u, zv = _gmlp_gate(m_ref[:, POOL_WIDTH:POOL_WIDTH + GMLP_WIDTH],
                       m_ref[:, POOL_WIDTH + GMLP_WIDTH:MIX_BLOCK], gn_ref[...])
    zv_ref[...] = zv
    yb_ref[...] = u * (wsd_ref[...] * zv + gb0_ref[...])

    x = jnp.concatenate([qk_ref[...], kv_ref[...]], axis=-1)
    conv = cw_ref[3:4, :] * x
    for i in range(CONV_W - 1):
        conv = conv + cw_ref[i:i + 1, :] * cst_ref[i]
    nconv_ref[0] = cst_ref[1]
    nconv_ref[1] = cst_ref[2]
    nconv_ref[2] = x
    act = _silu(conv)
    ba = zb_ref[:, Z_WIDTH:Z_WIDTH + 128]
    beta_all = jax.nn.sigmoid(ba)
    g_all = -jnp.exp(acol_ref[...]) * _softplus(ba + dtcol_ref[...])
    eg_all = jnp.exp(g_all)
    eg_ref[...] = eg_all
    lane = lax.broadcasted_iota(jnp.int32, (DEC_BATCH, 128), 1)
    qk_all = jnp.zeros((DEC_BATCH, 128), F32)
    for h in range(H):
        q = act[:, h * DN_DK:(h + 1) * DN_DK]
        k = act[:, (H + h) * DN_DK:(H + h + 1) * DN_DK]
        v = act[:, (2 * H + h) * DN_DK:(2 * H + h + 1) * DN_DK]
        q = q * lax.rsqrt(jnp.sum(q * q, axis=-1, keepdims=True) + EPS) * (DN_DK ** -0.5)
        k = k * lax.rsqrt(jnp.sum(k * k, axis=-1, keepdims=True) + EPS)
        beta = beta_all[:, h:h + 1]
        eg = eg_all[:, H + h:H + h + 1]
        vec_ref[:, h * 128:(h + 1) * 128] = k * (beta * eg)
        vec_ref[:, (H + h) * 128:(H + h + 1) * 128] = q * eg
        vec_ref[:, (2 * H + h) * 128:(2 * H + h + 1) * 128] = k
        vec_ref[:, (3 * H + h) * 128:(3 * H + h + 1) * 128] = v * beta
        qk = jnp.sum(q * k, axis=-1, keepdims=True)
        qk_all = jnp.where(lane == h, qk, qk_all)
    qk_out[...] = qk_all


def _sample_rows(mix, pool_t, conv_t, pool_w, pool_scale, gmlp_norm, ws_diag, gb0, conv_w,
                 acol, dtcol, ya, yb):
    rb = N_PROMPT // DEC_BATCH
    full = lambda shape: pl.BlockSpec(shape, lambda i: (0,) * len(shape))
    n = DEC_BATCH
    hbm = pl.BlockSpec(memory_space=pl.ANY)
    return pl.pallas_call(
        _sample_rows_kernel,
        grid=(1,),
        in_specs=[
            pl.BlockSpec((n, MIX_BLOCK), lambda i: (rb, 0)),
            pl.BlockSpec((n, MIX_BLOCK), lambda i: (rb, 1)),
            pl.BlockSpec((n, MIX_BLOCK), lambda i: (rb, 2)),
            pl.BlockSpec((n, MIX_BLOCK), lambda i: (rb, 3)),
            full((POOL_HIST, n, POOL_WIDTH)),
            full((CONV_W - 1, n, CONV_DIM)),
            full((4, GROUP, GROUP)),
            full((1, POOL_WIDTH)),
            full((1, GMLP_WIDTH)),
            full((1, GMLP_WIDTH)),
            full((1, GMLP_WIDTH)),
            full((CONV_W, CONV_DIM)),
            full((1, 128)),
            full((1, 128)),
            hbm,
            hbm,
        ],
        out_specs=[
            pl.BlockSpec((n, POOL_WIDTH), lambda i: (rb, 0)),
            pl.BlockSpec((n, GMLP_WIDTH), lambda i: (rb, 0)),
            full((n, GMLP_WIDTH)),
            full((POOL_HIST, n, POOL_WIDTH)),
            full((CONV_W - 1, n, CONV_DIM)),
            full((n, 4 * DN_HEADS * 128)),
            full((n, 128)),
            full((n, 128)),
        ],
        out_shape=[
            jax.ShapeDtypeStruct((N_ROWS, POOL_WIDTH), F32),
            jax.ShapeDtypeStruct((N_ROWS, GMLP_WIDTH), F32),
            jax.ShapeDtypeStruct((n, GMLP_WIDTH), F32),
            jax.ShapeDtypeStruct((POOL_HIST, n, POOL_WIDTH), F32),
            jax.ShapeDtypeStruct((CONV_W - 1, n, CONV_DIM), F32),
            jax.ShapeDtypeStruct((n, 4 * DN_HEADS * 128), F32),
            jax.ShapeDtypeStruct((n, 128), F32),
            jax.ShapeDtypeStruct((n, 128), F32),
        ],
        input_output_aliases={14: 0, 15: 1},
        compiler_params=pltpu.CompilerParams(
            dimension_semantics=("arbitrary",),
            vmem_limit_bytes=VMEM_LIMIT),
        name="sample_rows",
    )(mix, mix, mix, mix, pool_t, conv_t, pool_w, pool_scale, gmlp_norm,
      ws_diag, gb0, conv_w, acol, dtcol, ya, yb)


def _sample_state_kernel(vec_ref, scal_ref, z_ref, on_ref, s_ref, *rest):
    yc_ref, so_ref = rest[-2:]
    H = DN_HEADS
    row = lax.broadcasted_iota(jnp.int32, (1, 8, 1), 1)
    for h in range(H):
        s = s_ref[0, :, h]
        w = vec_ref[:, h:h + 1, :]
        qg = vec_ref[:, H + h:H + h + 1, :]
        k = vec_ref[:, 2 * H + h:2 * H + h + 1, :]
        vb = vec_ref[:, 3 * H + h:3 * H + h + 1, :]
        eg = scal_ref[:, h:h + 1, :]
        qk = scal_ref[:, H + h:H + h + 1, :]
        lhs = jnp.where(row == 0, w, jnp.where(row == 1, qg, 0.0))
        r = _bdot(lhs, s)
        v_new = vb - r[:, 0:1, :]
        o = r[:, 1:2, :] + qk * v_new
        k8 = jnp.where(row == 0, k, 0.0)
        v8 = jnp.broadcast_to(v_new, k8.shape)
        so_ref[0, :, h] = s * eg + _bdot_tn(k8, v8)
        o = o * lax.rsqrt(jnp.mean(o * o, axis=-1, keepdims=True) + EPS) * on_ref[...]
        yc_ref[:, h:h + 1, :] = o * _silu(z_ref[:, h:h + 1, :])


def _sample_state(layer, vec3, scal3, z3, onorm, state, stacked):
    sb = SAMPLE_BLOCK
    H = DN_HEADS
    in_specs = [
        pl.BlockSpec((sb, 4 * H, 128), lambda i: (i, 0, 0)),
        pl.BlockSpec((sb, 2 * H, 1), lambda i: (i, 0, 0)),
        pl.BlockSpec((sb, H, DN_DV), lambda i: (i, 0, 0)),
        pl.BlockSpec((1, DN_DV), lambda i: (0, 0)),
        pl.BlockSpec((1, sb, H, DN_DK, DN_DV), lambda i: (layer, i, 0, 0, 0)),
    ]
    args = [vec3, scal3, z3, onorm, state]
    aliases = {}
    if stacked is not None:
        in_specs.append(pl.BlockSpec(memory_space=pl.ANY))
        args.append(stacked)
        aliases = {5: 1}
    return pl.pallas_call(
        _sample_state_kernel,
        grid=(DEC_BATCH // sb,),
        in_specs=in_specs,
        out_specs=[
            pl.BlockSpec((sb, H, DN_DV), lambda i: (i, 0, 0)),
            pl.BlockSpec((1, sb, H, DN_DK, DN_DV), lambda i: (layer, i, 0, 0, 0)),
        ],
        out_shape=[
            jax.ShapeDtypeStruct((DEC_BATCH, H, DN_DV), F32),
            jax.ShapeDtypeStruct((DEPTH, DEC_BATCH, H, DN_DK, DN_DV), F32),
        ],
        input_output_aliases=aliases,
        compiler_params=pltpu.CompilerParams(
            dimension_semantics=("parallel",),
            vmem_limit_bytes=VMEM_LIMIT),
        name="sample_state",
    )(*args)


def _merge_kernel(x_ref, g0_ref, g1_ref, g2_ref, bg_ref, ya_ref, yb_ref, yc_ref,
                  pa_ref, pb_ref, pc_ref, wo_ref, *rest):
    o_ref = rest[-1]
    lane = lax.broadcasted_iota(jnp.int32, (1, GATE_WIN), 1)
    valid = (lane >= GATE_SHIFT) & (lane < GATE_SHIFT + D_MODEL)

    def branch(g_ref, i, y_ref, p_ref):
        gate = jax.nn.sigmoid(g_ref[...] + bg_ref[:, i * GATE_WIN:(i + 1) * GATE_WIN])
        gate = jnp.where(valid, gate, 0.0)
        return gate * jnp.dot(y_ref[...].astype(BF16), p_ref[...], preferred_element_type=F32)

    m = branch(g0_ref, 0, ya_ref, pa_ref)
    m = m + branch(g1_ref, 1, yb_ref, pb_ref)
    m = m + branch(g2_ref, 2, yc_ref, pc_ref)
    o_ref[...] = x_ref[...] + jnp.dot(m.astype(BF16), wo_ref[...], preferred_element_type=F32)


def _merge(x, gates, b_gate, ya, yb, yc, pa, pb, pc, wo, tile, first, count, prev):
    once = lambda shape: pl.BlockSpec(shape, lambda i: (0,) * len(shape),
                                      pipeline_mode=pl.Buffered(1))
    rows = lambda width: pl.BlockSpec((tile, width), lambda i: (first + i, 0))
    gate = lambda k: pl.BlockSpec((pl.Element(tile), pl.Element(GATE_WIN)),
                                  lambda i: ((first + i) * tile, COL_BA + k * D_MODEL))
    in_specs = [
        rows(D_MODEL),
        gate(0), gate(1), gate(2),
        once((1, 3 * GATE_WIN)),
        rows(POOL_WIDTH), rows(GMLP_WIDTH),
        pl.BlockSpec((tile, Z_WIDTH), lambda i: (i, 0)),
        once((POOL_WIDTH, GATE_WIN)),
        once((GMLP_WIDTH, GATE_WIN)),
        once((Z_WIDTH, GATE_WIN)),
        once((GATE_WIN, D_MODEL)),
    ]
    args = [x, gates, gates, gates, b_gate, ya, yb, yc, pa, pb, pc, wo]
    aliases = {}
    if prev is not None:
        in_specs.append(pl.BlockSpec(memory_space=pl.ANY))
        args.append(prev)
        aliases = {12: 0}
    return pl.pallas_call(
        _merge_kernel,
        grid=(count,),
        in_specs=in_specs,
        out_specs=rows(D_MODEL),
        out_shape=jax.ShapeDtypeStruct((N_ROWS, D_MODEL), F32),
        input_output_aliases=aliases,
        compiler_params=pltpu.CompilerParams(
            dimension_semantics=("parallel",),
            vmem_limit_bytes=VMEM_LIMIT),
        name="merge",
    )(*args)


def _final_kernel(x_ref, g_ref, o_ref):
    o_ref[...] = _rmsnorm(x_ref[...], g_ref[...])


def _final_norm(x, g, t, first, count):
    return pl.pallas_call(
        _final_kernel,
        grid=(count,),
        in_specs=[pl.BlockSpec((t, D_MODEL), lambda i: (first + i, 0)),
                  pl.BlockSpec((1, D_MODEL), lambda i: (0, 0))],
        out_specs=pl.BlockSpec((t, D_MODEL), lambda i: (i, 0)),
        out_shape=jax.ShapeDtypeStruct((count * t, D_MODEL), F32),
        compiler_params=pltpu.CompilerParams(dimension_semantics=("parallel",),
                                             vmem_limit_bytes=VMEM_LIMIT),
        name="final_norm",
    )(x, g)


def _lane_pad(v, offset, width=128):
    return jnp.zeros((1, width), F32).at[0, offset:offset + v.shape[0]].set(v)


def _tail_rows(mix, count, col, width):
    return jnp.stack([
        lax.slice(mix, ((b + 1) * SEQ - count, col), ((b + 1) * SEQ, col + width))
        for b in range(BATCH)])


def kernel(x_prompt, x_sample, state_delta, state_conv, state_pool, ffn1_norm, ffn1_wg, ffn1_wu, ffn1_wd, mix_norm, w_in, b_gate, pool_w, pool_scale, gmlp_norm, gmlp_ws, gmlp_b, dn_conv, dn_a_log, dn_dt_bias, dn_onorm, proj_a, proj_b, proj_c, w_o, ffn2_norm, ffn2_wg, ffn2_wu, ffn2_wd, final_norm):
    H = DN_HEADS
    x = jnp.concatenate([x_prompt.reshape(N_PROMPT, D_MODEL),
                         x_sample.reshape(DEC_BATCH, D_MODEL)], axis=0)
    causal = jnp.tril(jnp.ones((GMLP_CHUNK, GMLP_CHUNK), bool))
    outs_delta_p, outs_conv_p, outs_conv_s = [], [], []
    outs_pool_p, outs_pool_s, outs_zv = [], [], []
    delta_s = None
    shift_cols = ((0, 0), (GATE_SHIFT, GATE_WIN - D_MODEL - GATE_SHIFT))
    for l in range(DEPTH):
        ffn_tiles = dict(tm=1040, tf=256) if l < 2 else dict(tm=832, tf=512)
        x = _ffn(l, x, ffn1_norm[l][None], ffn1_wg, ffn1_wu, ffn1_wd, **ffn_tiles)

        mix = _inproj(l, x, mix_norm[l][None], w_in)

        pw = pool_w[l].astype(BF16)
        ps = pool_scale[l][None]
        gn = gmlp_norm[l][None]
        ws_tril = jnp.where(causal, gmlp_ws[l], 0).astype(BF16)
        gb_full = jnp.repeat(gmlp_b[l].T, GROUP, axis=1)
        ws_diag = jnp.repeat(gmlp_ws[l][:, 0, 0], GROUP)[None]
        gb0 = jnp.repeat(gmlp_b[l][:, 0], GROUP)[None]
        acol = _lane_pad(dn_a_log[l], H)
        dtcol = _lane_pad(dn_dt_bias[l], H)
        pair_rows = lambda v: jnp.pad(jnp.repeat(v.reshape(H // 2, 2), DN_CHUNK, axis=1),
                                      ((0, 8 - H // 2), (0, 0)))
        arow = pair_rows(dn_a_log[l])
        dtrow = pair_rows(dn_dt_bias[l])
        onorm = dn_onorm[l][None]

        ya, yb = _pool_gmlp(mix, pw, ps, gn, ws_tril, gb_full)
        a_p = lax.slice(mix, (0, COL_BA + H), (N_PROMPT, COL_BA + 2 * H))
        bat = a_p.reshape(N_PROMPT // DN_CHUNK, DN_CHUNK, H // 2, 2).transpose(0, 2, 3, 1)
        bat = jnp.pad(bat.reshape(N_PROMPT // DN_CHUNK, H // 2, 2 * DN_CHUNK),
                      ((0, 0), (0, 8 - H // 2), (0, 0)))
        yc_p, s_p = _delta(mix, bat, dn_conv[l], acol, dtcol, arow, dtrow, onorm)
        outs_delta_p.append(s_p)
        outs_conv_p.append(_tail_rows(mix, CONV_W - 1, COL_QKV, CONV_DIM))
        outs_pool_p.append(_tail_rows(mix, POOL_HIST, 0, POOL_WIDTH))

        pool_t = state_pool[l].transpose(1, 0, 2)
        conv_t = state_conv[l].transpose(1, 0, 2)
        ya, yb, zv_s, npool_t, nconv_t, vec, eg_all, qk_all = _sample_rows(
            mix, pool_t, conv_t, pw, ps, gn, ws_diag, gb0, dn_conv[l], acol, dtcol, ya, yb)
        vec3 = vec.reshape(DEC_BATCH, 4 * H, 128)
        scal3 = jnp.concatenate([eg_all[:, H:2 * H], qk_all[:, :H]], axis=1)[:, :, None]
        z3 = lax.slice(mix, (N_PROMPT, COL_Z), (N_ROWS, COL_Z + Z_WIDTH)).reshape(DEC_BATCH, H, DN_DV)
        yc_s3, delta_s = _sample_state(l, vec3, scal3, z3, onorm, state_delta, delta_s)
        outs_conv_s.append(nconv_t.transpose(1, 0, 2))
        outs_pool_s.append(npool_t.transpose(1, 0, 2))
        outs_zv.append(zv_s[:, None, :])

        weights = (jnp.pad(proj_a[l].astype(BF16), shift_cols),
                   jnp.pad(proj_b[l].astype(BF16), shift_cols),
                   jnp.pad(proj_c[l].astype(BF16), shift_cols),
                   jnp.pad(w_o[l].astype(BF16), shift_cols[::-1]))
        bg = jnp.pad(b_gate[l].reshape(3, D_MODEL), shift_cols).reshape(1, 3 * GATE_WIN)
        x_new = _merge(x, mix, bg, ya, yb, yc_p, *weights,
                       tile=MERGE_TILE, first=0, count=N_PROMPT // MERGE_TILE, prev=None)
        x = _merge(x, mix, bg, ya, yb, yc_s3.reshape(DEC_BATCH, Z_WIDTH), *weights,
                   tile=DEC_BATCH, first=N_PROMPT // DEC_BATCH, count=1, prev=x_new)

        x = _ffn(l, x, ffn2_norm[l][None], ffn2_wg, ffn2_wu, ffn2_wd, **ffn_tiles)

    g = final_norm[None]
    y_prompt = _final_norm(x, g, NORM_TILE, 0, N_PROMPT // NORM_TILE).reshape(BATCH, SEQ, D_MODEL)
    y_sample = _final_norm(x, g, DEC_BATCH, N_PROMPT // DEC_BATCH, 1).reshape(
        DEC_BATCH, 1, D_MODEL)
    return (y_prompt, y_sample, jnp.stack(outs_delta_p), delta_s,
            jnp.stack(outs_conv_p), jnp.stack(outs_conv_s), jnp.stack(outs_pool_p),
            jnp.stack(outs_pool_s), jnp.stack(outs_zv))
```

```python
import jax
import jax.numpy as jnp
from jax import lax
from jax.experimental import pallas as pl
from jax.experimental.pallas import tpu as pltpu

F32 = jnp.float32
BF16 = jnp.bfloat16

D_MODEL = 2048
BATCH = 4
SEQ = 2048
DEPTH = 4
DEC_BATCH = 128
EPS = 1e-6
FFN_DIM = 5632
POOL_WINDOWS = (2, 4, 8, 16)
POOL_WIDTH = 512
POOL_HIST = 15
GMLP_CHUNK = 128
GMLP_WIDTH = 512
DN_HEADS = 8
DN_DK = 128
DN_DV = 128
DN_CHUNK = 64
CONV_W = 4
CONV_DIM = 3072
GROUP = 128

N_PROMPT = BATCH * SEQ
N_ROWS = N_PROMPT + DEC_BATCH

COL_QKV = 1536
COL_Z = 4608
COL_BA = 5632
COL_GATE = 5648
IN_COLS = 11792
PROJ_COLS = 12288
MIX_BLOCK = 1536
Z_WIDTH = DN_HEADS * DN_DV
GATE_SHIFT = COL_GATE - COL_BA
GATE_WIN = D_MODEL + 128

ROW_TILE = 1040
MERGE_TILE = 256
FFN_TILE = 256
PROJ_ROWS = 2080
PROJ_TILE = 512
NORM_TILE = 512
DELTA_ROWS = 128
SAMPLE_BLOCK = 8
VMEM_LIMIT = 56 * 1024 * 1024
FFN_VMEM_LIMIT = 60 * 1024 * 1024


def _rmsnorm(x, g):
    ms = jnp.mean(x * x, axis=-1, keepdims=True)
    return x * lax.rsqrt(ms + EPS) * g


def _silu(x):
    return x * jax.nn.sigmoid(x)


def _softplus(x):
    return jnp.maximum(x, 0.0) + jnp.log1p(jnp.exp(-jnp.abs(x)))


def _bdot(a, b):
    return jnp.einsum('hmk,hkn->hmn', a.astype(BF16), b.astype(BF16),
                      preferred_element_type=F32)


def _bdot_tn(a, b):
    return jnp.einsum('hcm,hcn->hmn', a.astype(BF16), b.astype(BF16),
                      preferred_element_type=F32)


def _split(x):
    hi = x.astype(BF16)
    lo = (x - hi.astype(F32)).astype(BF16)
    return hi, lo


def _ffn_kernel(x_ref, g_ref, wg_ref, wu_ref, wd_ref, *rest):
    emit_norm = len(rest) == 4
    if emit_norm:
        gn_ref, o_ref, n_ref, h_ref = rest
    else:
        o_ref, h_ref = rest
    j = pl.program_id(1)

    @pl.when(j == 0)
    def _():
        x = x_ref[...]
        h_ref[...] = _rmsnorm(x, g_ref[...]).astype(BF16)
        o_ref[...] = x

    h = h_ref[...]
    a = jnp.dot(h, wg_ref[...].astype(BF16), preferred_element_type=F32)
    b = jnp.dot(h, wu_ref[...].astype(BF16), preferred_element_type=F32)
    act = (_silu(a) * (0.5 * b)).astype(BF16)
    o_ref[...] += jnp.dot(act, wd_ref[...].astype(BF16), preferred_element_type=F32)

    if emit_norm:
        @pl.when(j == pl.num_programs(1) - 1)
        def _():
            n_ref[...] = _rmsnorm(o_ref[...], gn_ref[...]).astype(BF16)


def _ffn(layer, x, g, wg, wu, wd, next_gain=None, tm=ROW_TILE, tf=FFN_TILE, x_buffers=2):
    rows = x.shape[0]
    vec = pl.BlockSpec((1, D_MODEL), lambda i, j: (0, 0))
    tile = pl.BlockSpec((tm, D_MODEL), lambda i, j: (i, 0))
    in_specs = [
        pl.BlockSpec((tm, D_MODEL), lambda i, j: (i, 0), pipeline_mode=pl.Buffered(x_buffers)),
        vec,
        pl.BlockSpec((None, D_MODEL, tf), lambda i, j: (layer, 0, j)),
        pl.BlockSpec((None, D_MODEL, tf), lambda i, j: (layer, 0, j)),
        pl.BlockSpec((None, tf, D_MODEL), lambda i, j: (layer, j, 0)),
    ]
    args = [x, g, wg, wu, wd]
    out_specs = [tile]
    out_shape = [jax.ShapeDtypeStruct((rows, D_MODEL), F32)]
    if next_gain is not None:
        in_specs.append(vec)
        args.append(next_gain)
        out_specs.append(pl.BlockSpec((tm, D_MODEL), lambda i, j: (i, 0),
                                      pipeline_mode=pl.Buffered(1)))
        out_shape.append(jax.ShapeDtypeStruct((rows, D_MODEL), BF16))
    out = pl.pallas_call(
        _ffn_kernel,
        grid=(rows // tm, FFN_DIM // tf),
        in_specs=in_specs,
        out_specs=out_specs,
        out_shape=out_shape,
        scratch_shapes=[pltpu.VMEM((tm, D_MODEL), BF16)],
        compiler_params=pltpu.CompilerParams(
            dimension_semantics=("parallel", "arbitrary"),
            vmem_limit_bytes=FFN_VMEM_LIMIT),
        name="ffn",
    )(*args)
    return out if next_gain is not None else out[0]


def _inproj_kernel(h_ref, w_ref, o_ref):
    col = pl.program_id(1) * PROJ_TILE + lax.broadcasted_iota(jnp.int32, (1, PROJ_TILE), 1)
    w = jnp.where(col < IN_COLS, w_ref[...], 0.0).astype(BF16)
    o_ref[...] = jnp.dot(h_ref[...], w, preferred_element_type=F32)


def _inproj(layer, h, w):
    rows = h.shape[0]
    tm = PROJ_ROWS
    return pl.pallas_call(
        _inproj_kernel,
        grid=(rows // tm, PROJ_COLS // PROJ_TILE),
        in_specs=[
            pl.BlockSpec((tm, D_MODEL), lambda i, j: (i, 0)),
            pl.BlockSpec((None, D_MODEL, PROJ_TILE), lambda i, j: (layer, 0, j)),
        ],
        out_specs=pl.BlockSpec((tm, PROJ_TILE), lambda i, j: (i, j)),
        out_shape=jax.ShapeDtypeStruct((rows, PROJ_COLS), F32),
        compiler_params=pltpu.CompilerParams(
            dimension_semantics=("parallel", "arbitrary"),
            vmem_limit_bytes=VMEM_LIMIT),
        name="inproj",
    )(h, w)


def _gmlp_gate(gu, gv, norm_g):
    u = jax.nn.gelu(gu)
    zv = _rmsnorm(jax.nn.gelu(gv), norm_g)
    return u, zv


def _pool_gmlp_kernel(m_ref, pw_ref, ps_ref, gn_ref, ws_ref, gb_ref,
                      ya_ref, yb_ref, ext_ref):
    c = pl.program_id(1)
    hist = 16
    rows = GMLP_CHUNK

    @pl.when(c == 0)
    def _():
        ext_ref[0:hist, :] = jnp.zeros((hist, POOL_WIDTH), F32)

    a = m_ref[:, 0:POOL_WIDTH]
    ext_ref[hist:hist + rows, :] = a
    pos = c * rows + lax.broadcasted_iota(jnp.int32, (rows, 1), 0)
    for gi, w in enumerate(POOL_WINDOWS):
        sl = slice(gi * GROUP, (gi + 1) * GROUP)
        s = a[:, sl]
        for i in range(1, w):
            s = s + ext_ref[hist - i:hist - i + rows, sl]
        cnt = jnp.minimum(pos + 1, w).astype(F32)
        d = s / cnt - a[:, sl]
        y = jnp.dot(d.astype(BF16), pw_ref[gi], preferred_element_type=F32)
        ya_ref[:, sl] = y * ps_ref[:, sl]
    ext_ref[0:hist, :] = a[rows - hist:rows, :]

    u, zv = _gmlp_gate(m_ref[:, POOL_WIDTH:POOL_WIDTH + GMLP_WIDTH],
                       m_ref[:, POOL_WIDTH + GMLP_WIDTH:MIX_BLOCK], gn_ref[...])
    for gi in range(4):
        sl = slice(gi * GROUP, (gi + 1) * GROUP)
        s = jnp.dot(ws_ref[gi], zv[:, sl].astype(BF16), preferred_element_type=F32)
        yb_ref[:, sl] = u[:, sl] * (s + gb_ref[:, sl])


def _pool_gmlp(mix, pool_w, pool_scale, gmlp_norm, ws_tril, gb_full):
    chunks = SEQ // GMLP_CHUNK
    row = lambda b, c: b * chunks + c
    full = lambda shape: pl.BlockSpec(shape, lambda b, c: (0,) * len(shape))
    return pl.pallas_call(
        _pool_gmlp_kernel,
        grid=(BATCH, chunks),
        in_specs=[
            pl.BlockSpec((GMLP_CHUNK, MIX_BLOCK), lambda b, c: (row(b, c), 0)),
            full((4, GROUP, GROUP)),
            full((1, POOL_WIDTH)),
            full((1, GMLP_WIDTH)),
            full((4, GMLP_CHUNK, GMLP_CHUNK)),
            full((GMLP_CHUNK, GMLP_WIDTH)),
        ],
        out_specs=[
            pl.BlockSpec((GMLP_CHUNK, POOL_WIDTH), lambda b, c: (row(b, c), 0)),
            pl.BlockSpec((GMLP_CHUNK, GMLP_WIDTH), lambda b, c: (row(b, c), 0)),
        ],
        out_shape=[
            jax.ShapeDtypeStruct((N_ROWS, POOL_WIDTH), F32),
            jax.ShapeDtypeStruct((N_ROWS, GMLP_WIDTH), F32),
        ],
        scratch_shapes=[pltpu.VMEM((16 + GMLP_CHUNK, POOL_WIDTH), F32)],
        compiler_params=pltpu.CompilerParams(
            dimension_semantics=("parallel", "arbitrary"),
            vmem_limit_bytes=VMEM_LIMIT),
        name="pool_gmlp",
    )(mix, pool_w, pool_scale, gmlp_norm, ws_tril, gb_full)


def _pair_bd(y, left):
    zero = jnp.zeros_like(y)
    return jnp.concatenate([jnp.where(left, y, zero), jnp.where(left, zero, y)],
                           axis=1).astype(BF16)


def _pair_mm(x, y_bd):
    return _bdot(x, y_bd)


def _pair_inverse(m, left, blk, eye):
    d = jnp.where(blk, m, 0.0)
    low = m - d
    d2 = _pair_mm(d, _pair_bd(d, left))
    d2_bd = _pair_bd(d2, left)
    d4 = _pair_mm(d2, d2_bd)
    d4_bd = _pair_bd(d4, left)
    d8_bd = _pair_bd(_pair_mm(d4, d4_bd), left)
    td = eye - d
    td = td + _pair_mm(td, d2_bd)
    td = td + _pair_mm(td, d4_bd)
    td = td + _pair_mm(td, d8_bd)
    n = _pair_mm(td, _pair_bd(low, left))
    n2 = _pair_mm(n, _pair_bd(n, left))
    x = eye - n + n2 - _pair_mm(n, _pair_bd(n2, left))
    t0 = _pair_mm(x, _pair_bd(td, left))
    th, tl = _split(t0)
    mh, ml = _split(m)
    th_bd = _pair_bd(th, left)
    tl_bd = _pair_bd(tl, left)
    mt = _bdot(mh, th_bd) + (_bdot(mh, tl_bd) + _bdot(ml, th_bd))
    r = eye - t0 - mt
    return t0 + _bdot(th, _pair_bd(r, left))


def _cumsum_exact(g, tri, dims):
    g1 = g.astype(BF16)
    r1 = g - g1.astype(F32)
    g2 = r1.astype(BF16)
    g3 = (r1 - g2.astype(F32)).astype(BF16)
    if dims == 'rows':
        dot = lambda t: jnp.dot(tri, t, preferred_element_type=F32)
    else:
        dot = lambda t: jnp.dot(t, tri, preferred_element_type=F32)
    return dot(g1) + (dot(g2) + dot(g3))


def _delta_kernel(qk_ref, kv_ref, zb_ref, bat_ref, cw_ref, acol_ref, dtcol_ref,
                  arow_ref, dtrow_ref, on_ref, yc_ref, s_ref, ext_ref):
    n = pl.program_id(1)
    C = DN_CHUNK
    H = DN_HEADS
    R = DELTA_ROWS

    @pl.when(n == 0)
    def _():
        ext_ref[0:8, :] = jnp.zeros((8, CONV_DIM), F32)
        s_ref[...] = jnp.zeros(s_ref.shape, F32)

    ext_ref[8:8 + R, 0:MIX_BLOCK] = qk_ref[...]
    ext_ref[8:8 + R, MIX_BLOCK:CONV_DIM] = kv_ref[...]
    conv = cw_ref[3:4, :] * ext_ref[8:8 + R, :]
    for i in range(CONV_W - 1):
        conv = conv + cw_ref[i:i + 1, :] * ext_ref[5 + i:5 + i + R, :]
    ext_ref[0:8, :] = ext_ref[R:R + 8, :]
    act = _silu(conv)

    ri = lax.broadcasted_iota(jnp.int32, (C, 2 * C), 0)
    li = lax.broadcasted_iota(jnp.int32, (C, 2 * C), 1)
    ci = li % C
    left = li < C
    causal = ri >= ci
    strict = ri > ci
    blk = (ri // 16) == (ci // 16)
    eye = jnp.where(ri == ci, 1.0, 0.0).astype(F32)
    zero = jnp.zeros((C, DN_DK), F32)

    rr = lax.broadcasted_iota(jnp.int32, (R, R), 0)
    cc = lax.broadcasted_iota(jnp.int32, (R, R), 1)
    tril_chunks = jnp.where((rr // C == cc // C) & (rr >= cc), 1.0, 0.0).astype(BF16)
    pr = lax.broadcasted_iota(jnp.int32, (2 * C, 2 * C), 0)
    pc = lax.broadcasted_iota(jnp.int32, (2 * C, 2 * C), 1)
    triu_pair = jnp.where((pr // C == pc // C) & (pr <= pc), 1.0, 0.0).astype(BF16)

    ba = zb_ref[:, Z_WIDTH:Z_WIDTH + 128]
    beta_all = jax.nn.sigmoid(ba)
    g_all = -jnp.exp(acol_ref[...]) * _softplus(ba + dtcol_ref[...])
    gc_all = _cumsum_exact(g_all, tril_chunks, 'rows')
    on = on_ref[...]

    NC = R // C
    NP = H // 2
    lane2 = lambda a, b: jnp.concatenate([a, b], axis=1)
    diag2 = lambda a, b: jnp.concatenate([lane2(a, jnp.zeros_like(b)),
                                          lane2(jnp.zeros_like(a), b)], axis=0)

    ks, kbs, qes, gcs = [], [], [], []
    kq_lhs, kq_rhs, gc_pair, gr_pair, uw_rhs = [], [], [], [], []
    for c in range(NC):
        rows = slice(c * C, (c + 1) * C)
        heads = lambda base: jnp.stack(
            [act[rows, (base + h) * DN_DK:(base + h + 1) * DN_DK] for h in range(H)])
        q = heads(0)
        k = heads(H)
        v = heads(2 * H)
        q = q * lax.rsqrt(jnp.sum(q * q, axis=-1, keepdims=True) + EPS) * (DN_DK ** -0.5)
        k = k * lax.rsqrt(jnp.sum(k * k, axis=-1, keepdims=True) + EPS)
        beta = jnp.stack([beta_all[rows, h:h + 1] for h in range(H)])
        gc = jnp.stack([gc_all[rows, H + h:H + h + 1] for h in range(H)])
        egc = jnp.exp(gc)
        kb = k * beta
        vb = v * beta
        kbe = kb * egc
        ks.append(k)
        kbs.append(kb)
        qes.append(q * egc)
        gcs.append(gc)
        g_t = -jnp.exp(arow_ref[...]) * _softplus(bat_ref[c] + dtrow_ref[...])
        gr_all = _cumsum_exact(g_t, triu_pair, 'cols')
        for p in range(NP):
            a, b = 2 * p, 2 * p + 1
            kq_lhs.append(jnp.concatenate([lane2(kb[a], kb[b]), lane2(q[a], q[b])], axis=0))
            kq_rhs.append(diag2(k[a], k[b]))
            gc_pair.append(jnp.where(left, gc[a], gc[b]))
            gr_pair.append(gr_all[p:p + 1, :])
            uw_rhs.append(diag2(lane2(vb[a], kbe[a]), lane2(vb[b], kbe[b])))

    kq = jnp.einsum('bmk,bnk->bmn', jnp.stack(kq_lhs).astype(BF16), jnp.stack(kq_rhs).astype(BF16),
                    preferred_element_type=F32)
    diff = jnp.stack(gc_pair) - jnp.stack(gr_pair)
    decay = jnp.where(causal, jnp.exp(jnp.where(causal, diff, 0.0)), 0.0)
    m = jnp.where(strict, kq[:, 0:C] * decay, 0.0)
    a_mat = kq[:, C:2 * C] * decay
    t = _pair_inverse(m, left, blk, eye)
    uw = _bdot(t, jnp.stack(uw_rhs))

    for c in range(NC):
        rows = slice(c * C, (c + 1) * C)
        part = lambda h, i: uw[c * NP + h // 2][:, (2 * (h % 2) + i) * DN_DV:
                                               (2 * (h % 2) + i + 1) * DN_DV]
        u = jnp.stack([part(h, 0) for h in range(H)])
        w = jnp.stack([part(h, 1) for h in range(H)])
        s = s_ref[0]
        ws_qs = _bdot(jnp.concatenate([w, qes[c]], axis=1), s)
        v_new = u - ws_qs[:, 0:C]
        glast = gcs[c][:, C - 1:C, :]
        kg = ks[c] * jnp.exp(glast - gcs[c])
        s_ref[0] = s * jnp.exp(glast) + _bdot_tn(kg, v_new)
        vn_bd = jnp.stack([diag2(v_new[2 * p], v_new[2 * p + 1]) for p in range(NP)])
        av = _bdot(a_mat[c * NP:(c + 1) * NP], vn_bd)
        o = ws_qs[:, C:2 * C] + jnp.stack(
            [av[h // 2][:, (h % 2) * DN_DV:(h % 2 + 1) * DN_DV] for h in range(H)])
        o = o * lax.rsqrt(jnp.mean(o * o, axis=-1, keepdims=True) + EPS) * on
        for h in range(H):
            sl = slice(h * DN_DV, (h + 1) * DN_DV)
            yc_ref[rows, sl] = o[h] * _silu(zb_ref[rows, sl])


def _delta(mix, bat, conv_w, acol, dtcol, arow, dtrow, onorm):
    steps = SEQ // DELTA_ROWS
    cps = DELTA_ROWS // DN_CHUNK
    row = lambda b, n: b * steps + n
    full = lambda shape: pl.BlockSpec(shape, lambda b, n: (0,) * len(shape))
    return pl.pallas_call(
        _delta_kernel,
        grid=(BATCH, steps),
        in_specs=[
            pl.BlockSpec((DELTA_ROWS, MIX_BLOCK), lambda b, n: (row(b, n), 1)),
            pl.BlockSpec((DELTA_ROWS, MIX_BLOCK), lambda b, n: (row(b, n), 2)),
            pl.BlockSpec((DELTA_ROWS, MIX_BLOCK), lambda b, n: (row(b, n), 3)),
            pl.BlockSpec((cps, 8, 2 * DN_CHUNK), lambda b, n: (row(b, n), 0, 0)),
            full((CONV_W, CONV_DIM)),
            full((1, 128)),
            full((1, 128)),
            full((8, 2 * DN_CHUNK)),
            full((8, 2 * DN_CHUNK)),
            full((1, DN_DV)),
        ],
        out_specs=[
            pl.BlockSpec((DELTA_ROWS, Z_WIDTH), lambda b, n: (row(b, n), 0)),
            pl.BlockSpec((1, DN_HEADS, DN_DK, DN_DV), lambda b, n: (b, 0, 0, 0)),
        ],
        out_shape=[
            jax.ShapeDtypeStruct((N_PROMPT, Z_WIDTH), F32),
            jax.ShapeDtypeStruct((BATCH, DN_HEADS, DN_DK, DN_DV), F32),
        ],
        scratch_shapes=[pltpu.VMEM((8 + DELTA_ROWS, CONV_DIM), F32)],
        compiler_params=pltpu.CompilerParams(
            dimension_semantics=("parallel", "arbitrary"),
            vmem_limit_bytes=VMEM_LIMIT),
        name="delta",
    )(mix, mix, mix, bat, conv_w, acol, dtcol, arow, dtrow, onorm)


def _sample_rows_kernel(m_ref, qk_ref, kv_ref, zb_ref, pool_ref, cst_ref,
                        pw_ref, ps_ref, gn_ref, wsd_ref, gb0_ref, cw_ref, acol_ref, dtcol_ref,
                        ya_in, yb_in,
                        ya_ref, yb_ref, zv_ref, npool_ref, nconv_ref, vec_ref, eg_ref, qk_out):
    del ya_in, yb_in
    H = DN_HEADS
    a = m_ref[:, 0:POOL_WIDTH]
    for gi, w in enumerate(POOL_WINDOWS):
        sl = slice(gi * GROUP, (gi + 1) * GROUP)
        s = a[:, sl]
        for i in range(1, w):
            s = s + pool_ref[POOL_HIST - i, :, sl]
        d = s / float(w) - a[:, sl]
        y = jnp.dot(d.astype(BF16), pw_ref[gi], preferred_element_type=F32)
        ya_ref[:, sl] = y * ps_ref[:, sl]
    for i in range(POOL_HIST - 1):
        npool_ref[i] = pool_ref[i + 1]
    npool_ref[POOL_HIST - 1] = a

    u, zv = _gmlp_gate(m_ref[:, POOL_WIDTH:POOL_WIDTH + GMLP_WIDTH],
                       m_ref[:, POOL_WIDTH + GMLP_WIDTH:MIX_BLOCK], gn_ref[...])
    zv_ref[...] = zv
    yb_ref[...] = u * (wsd_ref[...] * zv + gb0_ref[...])

    x = jnp.concatenate([qk_ref[...], kv_ref[...]], axis=-1)
    conv = cw_ref[3:4, :] * x
    for i in range(CONV_W - 1):
        conv = conv + cw_ref[i:i + 1, :] * cst_ref[i]
    nconv_ref[0] = cst_ref[1]
    nconv_ref[1] = cst_ref[2]
    nconv_ref[2] = x
    act = _silu(conv)
    ba = zb_ref[:, Z_WIDTH:Z_WIDTH + 128]
    beta_all = jax.nn.sigmoid(ba)
    g_all = -jnp.exp(acol_ref[...]) * _softplus(ba + dtcol_ref[...])
    eg_all = jnp.exp(g_all)
    eg_ref[...] = eg_all
    lane = lax.broadcasted_iota(jnp.int32, (DEC_BATCH, 128), 1)
    qk_all = jnp.zeros((DEC_BATCH, 128), F32)
    for h in range(H):
        q = act[:, h * DN_DK:(h + 1) * DN_DK]
        k = act[:, (H + h) * DN_DK:(H + h + 1) * DN_DK]
        v = act[:, (2 * H + h) * DN_DK:(2 * H + h + 1) * DN_DK]
        q = q * lax.rsqrt(jnp.sum(q * q, axis=-1, keepdims=True) + EPS) * (DN_DK ** -0.5)
        k = k * lax.rsqrt(jnp.sum(k * k, axis=-1, keepdims=True) + EPS)
        beta = beta_all[:, h:h + 1]
        eg = eg_all[:, H + h:H + h + 1]
        vec_ref[:, h * 128:(h + 1) * 128] = k * (beta * eg)
        vec_ref[:, (H + h) * 128:(H + h + 1) * 128] = q * eg
        vec_ref[:, (2 * H + h) * 128:(2 * H + h + 1) * 128] = k
        vec_ref[:, (3 * H + h) * 128:(3 * H + h + 1) * 128] = v * beta
        qk = jnp.sum(q * k, axis=-1, keepdims=True)
        qk_all = jnp.where(lane == h, qk, qk_all)
    qk_out[...] = qk_all


def _sample_rows(mix, pool_t, conv_t, pool_w, pool_scale, gmlp_norm, ws_diag, gb0, conv_w,
                 acol, dtcol, ya, yb):
    rb = N_PROMPT // DEC_BATCH
    full = lambda shape: pl.BlockSpec(shape, lambda i: (0,) * len(shape))
    n = DEC_BATCH
    hbm = pl.BlockSpec(memory_space=pl.ANY)
    return pl.pallas_call(
        _sample_rows_kernel,
        grid=(1,),
        in_specs=[
            pl.BlockSpec((n, MIX_BLOCK), lambda i: (rb, 0)),
            pl.BlockSpec((n, MIX_BLOCK), lambda i: (rb, 1)),
            pl.BlockSpec((n, MIX_BLOCK), lambda i: (rb, 2)),
            pl.BlockSpec((n, MIX_BLOCK), lambda i: (rb, 3)),
            full((POOL_HIST, n, POOL_WIDTH)),
            full((CONV_W - 1, n, CONV_DIM)),
            full((4, GROUP, GROUP)),
            full((1, POOL_WIDTH)),
            full((1, GMLP_WIDTH)),
            full((1, GMLP_WIDTH)),
            full((1, GMLP_WIDTH)),
            full((CONV_W, CONV_DIM)),
            full((1, 128)),
            full((1, 128)),
            hbm,
            hbm,
        ],
        out_specs=[
            pl.BlockSpec((n, POOL_WIDTH), lambda i: (rb, 0)),
            pl.BlockSpec((n, GMLP_WIDTH), lambda i: (rb, 0)),
            full((n, GMLP_WIDTH)),
            full((POOL_HIST, n, POOL_WIDTH)),
            full((CONV_W - 1, n, CONV_DIM)),
            full((n, 4 * DN_HEADS * 128)),
            full((n, 128)),
            full((n, 128)),
        ],
        out_shape=[
            jax.ShapeDtypeStruct((N_ROWS, POOL_WIDTH), F32),
            jax.ShapeDtypeStruct((N_ROWS, GMLP_WIDTH), F32),
            jax.ShapeDtypeStruct((n, GMLP_WIDTH), F32),
            jax.ShapeDtypeStruct((POOL_HIST, n, POOL_WIDTH), F32),
            jax.ShapeDtypeStruct((CONV_W - 1, n, CONV_DIM), F32),
            jax.ShapeDtypeStruct((n, 4 * DN_HEADS * 128), F32),
            jax.ShapeDtypeStruct((n, 128), F32),
            jax.ShapeDtypeStruct((n, 128), F32),
        ],
        input_output_aliases={14: 0, 15: 1},
        compiler_params=pltpu.CompilerParams(
            dimension_semantics=("arbitrary",),
            vmem_limit_bytes=VMEM_LIMIT),
        name="sample_rows",
    )(mix, mix, mix, mix, pool_t, conv_t, pool_w, pool_scale, gmlp_norm,
      ws_diag, gb0, conv_w, acol, dtcol, ya, yb)


def _sample_state_kernel(vec_ref, scal_ref, z_ref, on_ref, s_ref, *rest):
    yc_ref, so_ref = rest[-2:]
    H = DN_HEADS
    row = lax.broadcasted_iota(jnp.int32, (1, 8, 1), 1)
    for h in range(H):
        s = s_ref[0, :, h]
        w = vec_ref[:, h:h + 1, :]
        qg = vec_ref[:, H + h:H + h + 1, :]
        k = vec_ref[:, 2 * H + h:2 * H + h + 1, :]
        vb = vec_ref[:, 3 * H + h:3 * H + h + 1, :]
        eg = scal_ref[:, h:h + 1, :]
        qk = scal_ref[:, H + h:H + h + 1, :]
        lhs = jnp.where(row == 0, w, jnp.where(row == 1, qg, 0.0))
        r = _bdot(lhs, s)
        v_new = vb - r[:, 0:1, :]
        o = r[:, 1:2, :] + qk * v_new
        k8 = jnp.where(row == 0, k, 0.0)
        v8 = jnp.broadcast_to(v_new, k8.shape)
        so_ref[0, :, h] = s * eg + _bdot_tn(k8, v8)
        o = o * lax.rsqrt(jnp.mean(o * o, axis=-1, keepdims=True) + EPS) * on_ref[...]
        yc_ref[:, h:h + 1, :] = o * _silu(z_ref[:, h:h + 1, :])


def _sample_state(layer, vec3, scal3, z3, onorm, state, stacked):
    sb = SAMPLE_BLOCK
    H = DN_HEADS
    in_specs = [
        pl.BlockSpec((sb, 4 * H, 128), lambda i: (i, 0, 0)),
        pl.BlockSpec((sb, 2 * H, 1), lambda i: (i, 0, 0)),
        pl.BlockSpec((sb, H, DN_DV), lambda i: (i, 0, 0)),
        pl.BlockSpec((1, DN_DV), lambda i: (0, 0)),
        pl.BlockSpec((1, sb, H, DN_DK, DN_DV), lambda i: (layer, i, 0, 0, 0)),
    ]
    args = [vec3, scal3, z3, onorm, state]
    aliases = {}
    if stacked is not None:
        in_specs.append(pl.BlockSpec(memory_space=pl.ANY))
        args.append(stacked)
        aliases = {5: 1}
    return pl.pallas_call(
        _sample_state_kernel,
        grid=(DEC_BATCH // sb,),
        in_specs=in_specs,
        out_specs=[
            pl.BlockSpec((sb, H, DN_DV), lambda i: (i, 0, 0)),
            pl.BlockSpec((1, sb, H, DN_DK, DN_DV), lambda i: (layer, i, 0, 0, 0)),
        ],
        out_shape=[
            jax.ShapeDtypeStruct((DEC_BATCH, H, DN_DV), F32),
            jax.ShapeDtypeStruct((DEPTH, DEC_BATCH, H, DN_DK, DN_DV), F32),
        ],
        input_output_aliases=aliases,
        compiler_params=pltpu.CompilerParams(
            dimension_semantics=("parallel",),
            vmem_limit_bytes=VMEM_LIMIT),
        name="sample_state",
    )(*args)


def _merge_kernel(x_ref, g0_ref, g1_ref, g2_ref, bg_ref, ya_ref, yb_ref, yc_ref,
                  pa_ref, pb_ref, pc_ref, wo_ref, *rest):
    o_ref = rest[-1]
    lane = lax.broadcasted_iota(jnp.int32, (1, GATE_WIN), 1)
    valid = (lane >= GATE_SHIFT) & (lane < GATE_SHIFT + D_MODEL)

    def branch(g_ref, i, y_ref, p_ref):
        gate = jax.nn.sigmoid(g_ref[...] + bg_ref[:, i * GATE_WIN:(i + 1) * GATE_WIN])
        gate = jnp.where(valid, gate, 0.0)
        return gate * jnp.dot(y_ref[...].astype(BF16), p_ref[...], preferred_element_type=F32)

    m = branch(g0_ref, 0, ya_ref, pa_ref)
    m = m + branch(g1_ref, 1, yb_ref, pb_ref)
    m = m + branch(g2_ref, 2, yc_ref, pc_ref)
    o_ref[...] = x_ref[...] + jnp.dot(m.astype(BF16), wo_ref[...], preferred_element_type=F32)


def _merge(x, proj, b_gate, ya, yb, yc, pa, pb, pc, wo, tile, first, count, prev):
    once = lambda shape: pl.BlockSpec(shape, lambda i: (0,) * len(shape),
                                      pipeline_mode=pl.Buffered(1))
    rows = lambda width: pl.BlockSpec((tile, width), lambda i: (first + i, 0))
    gate = lambda k: pl.BlockSpec((pl.Element(tile), pl.Element(GATE_WIN)),
                                  lambda i: ((first + i) * tile, COL_BA + k * D_MODEL))
    in_specs = [
        rows(D_MODEL),
        gate(0), gate(1), gate(2),
        once((1, 3 * GATE_WIN)),
        rows(POOL_WIDTH), rows(GMLP_WIDTH),
        pl.BlockSpec((tile, Z_WIDTH), lambda i: (i, 0)),
        once((POOL_WIDTH, GATE_WIN)),
        once((GMLP_WIDTH, GATE_WIN)),
        once((Z_WIDTH, GATE_WIN)),
        once((GATE_WIN, D_MODEL)),
    ]
    args = [x, proj, proj, proj, b_gate, ya, yb, yc, pa, pb, pc, wo]
    aliases = {}
    if prev is not None:
        in_specs.append(pl.BlockSpec(memory_space=pl.ANY))
        args.append(prev)
        aliases = {len(args) - 1: 0}
    return pl.pallas_call(
        _merge_kernel,
        grid=(count,),
        in_specs=in_specs,
        out_specs=rows(D_MODEL),
        out_shape=jax.ShapeDtypeStruct((N_ROWS, D_MODEL), F32),
        input_output_aliases=aliases,
        compiler_params=pltpu.CompilerParams(
            dimension_semantics=("parallel",),
            vmem_limit_bytes=VMEM_LIMIT),
        name="merge",
    )(*args)


def _final_kernel(x_ref, g_ref, o_ref):
    o_ref[...] = _rmsnorm(x_ref[...], g_ref[...])


def _final_norm(x, g, t, first, count):
    return pl.pallas_call(
        _final_kernel,
        grid=(count,),
        in_specs=[pl.BlockSpec((t, D_MODEL), lambda i: (first + i, 0)),
                  pl.BlockSpec((1, D_MODEL), lambda i: (0, 0))],
        out_specs=pl.BlockSpec((t, D_MODEL), lambda i: (i, 0)),
        out_shape=jax.ShapeDtypeStruct((count * t, D_MODEL), F32),
        compiler_params=pltpu.CompilerParams(dimension_semantics=("parallel",),
                                             vmem_limit_bytes=VMEM_LIMIT),
        name="final_norm",
    )(x, g)


def _lane_pad(v, offset, width=128):
    return jnp.zeros((1, width), F32).at[0, offset:offset + v.shape[0]].set(v)


def _tail_rows(mix, count, col, width):
    return jnp.stack([
        lax.slice(mix, ((b + 1) * SEQ - count, col), ((b + 1) * SEQ, col + width))
        for b in range(BATCH)])


def kernel(x_prompt, x_sample, state_delta, state_conv, state_pool, ffn1_norm, ffn1_wg, ffn1_wu, ffn1_wd, mix_norm, w_in, b_gate, pool_w, pool_scale, gmlp_norm, gmlp_ws, gmlp_b, dn_conv, dn_a_log, dn_dt_bias, dn_onorm, proj_a, proj_b, proj_c, w_o, ffn2_norm, ffn2_wg, ffn2_wu, ffn2_wd, final_norm):
    H = DN_HEADS
    x = jnp.concatenate([x_prompt.reshape(N_PROMPT, D_MODEL),
                         x_sample.reshape(DEC_BATCH, D_MODEL)], axis=0)
    causal = jnp.tril(jnp.ones((GMLP_CHUNK, GMLP_CHUNK), bool))
    outs_delta_p, outs_conv_p, outs_conv_s = [], [], []
    outs_pool_p, outs_pool_s, outs_zv = [], [], []
    delta_s = None
    shift_cols = ((0, 0), (GATE_SHIFT, GATE_WIN - D_MODEL - GATE_SHIFT))
    for l in range(DEPTH):
        x, h_mix = _ffn(l, x, ffn1_norm[l][None], ffn1_wg, ffn1_wu, ffn1_wd,
                        next_gain=mix_norm[l][None])

        mix = _inproj(l, h_mix, w_in)

        pw = pool_w[l].astype(BF16)
        ps = pool_scale[l][None]
        gn = gmlp_norm[l][None]
        ws_tril = jnp.where(causal, gmlp_ws[l], 0).astype(BF16)
        gb_full = jnp.repeat(gmlp_b[l].T, GROUP, axis=1)
        ws_diag = jnp.repeat(gmlp_ws[l][:, 0, 0], GROUP)[None]
        gb0 = jnp.repeat(gmlp_b[l][:, 0], GROUP)[None]
        acol = _lane_pad(dn_a_log[l], H)
        dtcol = _lane_pad(dn_dt_bias[l], H)
        pair_rows = lambda v: jnp.pad(jnp.repeat(v.reshape(H // 2, 2), DN_CHUNK, axis=1),
                                      ((0, 8 - H // 2), (0, 0)))
        arow = pair_rows(dn_a_log[l])
        dtrow = pair_rows(dn_dt_bias[l])
        onorm = dn_onorm[l][None]

        ya, yb = _pool_gmlp(mix, pw, ps, gn, ws_tril, gb_full)
        a_p = lax.slice(mix, (0, COL_BA + H), (N_PROMPT, COL_BA + 2 * H))
        bat = a_p.reshape(N_PROMPT // DN_CHUNK, DN_CHUNK, H // 2, 2).transpose(0, 2, 3, 1)
        bat = jnp.pad(bat.reshape(N_PROMPT // DN_CHUNK, H // 2, 2 * DN_CHUNK),
                      ((0, 0), (0, 8 - H // 2), (0, 0)))
        yc_p, s_p = _delta(mix, bat, dn_conv[l], acol, dtcol, arow, dtrow, onorm)
        outs_delta_p.append(s_p)
        outs_conv_p.append(_tail_rows(mix, CONV_W - 1, COL_QKV, CONV_DIM))
        outs_pool_p.append(_tail_rows(mix, POOL_HIST, 0, POOL_WIDTH))

        pool_t = state_pool[l].transpose(1, 0, 2)
        conv_t = state_conv[l].transpose(1, 0, 2)
        ya, yb, zv_s, npool_t, nconv_t, vec, eg_all, qk_all = _sample_rows(
            mix, pool_t, conv_t, pw, ps, gn, ws_diag, gb0, dn_conv[l], acol, dtcol, ya, yb)
        vec3 = vec.reshape(DEC_BATCH, 4 * H, 128)
        scal3 = jnp.concatenate([eg_all[:, H:2 * H], qk_all[:, :H]], axis=1)[:, :, None]
        z3 = lax.slice(mix, (N_PROMPT, COL_Z), (N_ROWS, COL_Z + Z_WIDTH)).reshape(DEC_BATCH, H, DN_DV)
        yc_s3, delta_s = _sample_state(l, vec3, scal3, z3, onorm, state_delta, delta_s)
        outs_conv_s.append(nconv_t.transpose(1, 0, 2))
        outs_pool_s.append(npool_t.transpose(1, 0, 2))
        outs_zv.append(zv_s[:, None, :])

        weights = (jnp.pad(proj_a[l].astype(BF16), shift_cols),
                   jnp.pad(proj_b[l].astype(BF16), shift_cols),
                   jnp.pad(proj_c[l].astype(BF16), shift_cols),
                   jnp.pad(w_o[l].astype(BF16), shift_cols[::-1]))
        bg = jnp.pad(b_gate[l].reshape(3, D_MODEL), shift_cols).reshape(1, 3 * GATE_WIN)
        x_new = _merge(x, mix, bg, ya, yb, yc_p, *weights,
                       tile=MERGE_TILE, first=0, count=N_PROMPT // MERGE_TILE, prev=None)
        x = _merge(x, mix, bg, ya, yb, yc_s3.reshape(DEC_BATCH, Z_WIDTH), *weights,
                   tile=DEC_BATCH, first=N_PROMPT // DEC_BATCH, count=1, prev=x_new)

        x = _ffn(l, x, ffn2_norm[l][None], ffn2_wg, ffn2_wu, ffn2_wd,
                 x_buffers=2 if l < 2 else 1)

    g = final_norm[None]
    y_prompt = _final_norm(x, g, NORM_TILE, 0, N_PROMPT // NORM_TILE).reshape(BATCH, SEQ, D_MODEL)
    y_sample = _final_norm(x, g, DEC_BATCH, N_PROMPT // DEC_BATCH, 1).reshape(
        DEC_BATCH, 1, D_MODEL)
    return (y_prompt, y_sample, jnp.stack(outs_delta_p), delta_s,
            jnp.stack(outs_conv_p), jnp.stack(outs_conv_s), jnp.stack(outs_pool_p),
            jnp.stack(outs_pool_s), jnp.stack(outs_zv))
```

```python
import jax
import jax.numpy as jnp
from jax import lax
from jax.experimental import pallas as pl
from jax.experimental.pallas import tpu as pltpu

F32 = jnp.float32
BF16 = jnp.bfloat16

D_MODEL = 2048
BATCH = 4
SEQ = 2048
DEPTH = 4
DEC_BATCH = 128
EPS = 1e-6
FFN_DIM = 5632
POOL_WINDOWS = (2, 4, 8, 16)
POOL_WIDTH = 512
POOL_HIST = 15
GMLP_CHUNK = 128
GMLP_WIDTH = 512
DN_HEADS = 8
DN_DK = 128
DN_DV = 128
DN_CHUNK = 64
CONV_W = 4
CONV_DIM = 3072
GROUP = 128

N_PROMPT = BATCH * SEQ
N_ROWS = N_PROMPT + DEC_BATCH

COL_QKV = 1536
COL_Z = 4608
COL_BA = 5632
COL_GATE = 5648
IN_COLS = 11792
PROJ_COLS = 12288
MIX_BLOCK = 1536
Z_WIDTH = DN_HEADS * DN_DV
GATE_SHIFT = COL_GATE - COL_BA
GATE_WIN = D_MODEL + 128

ROW_TILE = 1040
MERGE_TILE = 256
FFN_TILE = 256
PROJ_ROWS = 2080
PROJ_TILE = 512
NORM_TILE = 512
DELTA_ROWS = 128
POOL_ROWS = 512
SAMPLE_BLOCK = 8
VMEM_LIMIT = 56 * 1024 * 1024
FFN_VMEM_LIMIT = 60 * 1024 * 1024


def _rmsnorm(x, g):
    ms = jnp.mean(x * x, axis=-1, keepdims=True)
    return x * lax.rsqrt(ms + EPS) * g


def _silu(x):
    return x * jax.nn.sigmoid(x)


def _softplus(x):
    return jnp.maximum(x, 0.0) + jnp.log1p(jnp.exp(-jnp.abs(x)))


def _bdot(a, b):
    return jnp.einsum('hmk,hkn->hmn', a.astype(BF16), b.astype(BF16),
                      preferred_element_type=F32)


def _bdot_tn(a, b):
    return jnp.einsum('hcm,hcn->hmn', a.astype(BF16), b.astype(BF16),
                      preferred_element_type=F32)


def _split(x):
    hi = x.astype(BF16)
    lo = (x - hi.astype(F32)).astype(BF16)
    return hi, lo


def _ffn_kernel(x_ref, g_ref, wg_ref, wu_ref, wd_ref, *rest):
    emit_norm = len(rest) == 4
    if emit_norm:
        gn_ref, o_ref, n_ref, h_ref = rest
    else:
        o_ref, h_ref = rest
    j = pl.program_id(1)

    @pl.when(j == 0)
    def _():
        x = x_ref[...]
        h_ref[...] = _rmsnorm(x, g_ref[...]).astype(BF16)
        o_ref[...] = x

    h = h_ref[...]
    a = jnp.dot(h, wg_ref[...].astype(BF16), preferred_element_type=F32)
    b = jnp.dot(h, wu_ref[...].astype(BF16), preferred_element_type=F32)
    act = (_silu(a) * (0.5 * b)).astype(BF16)
    o_ref[...] += jnp.dot(act, wd_ref[...].astype(BF16), preferred_element_type=F32)

    if emit_norm:
        @pl.when(j == pl.num_programs(1) - 1)
        def _():
            n_ref[...] = _rmsnorm(o_ref[...], gn_ref[...]).astype(BF16)


def _ffn(layer, x, g, wg, wu, wd, next_gain=None, tm=ROW_TILE, tf=FFN_TILE, x_buffers=2):
    rows = x.shape[0]
    vec = pl.BlockSpec((1, D_MODEL), lambda i, j: (0, 0))
    tile = pl.BlockSpec((tm, D_MODEL), lambda i, j: (i, 0))
    in_specs = [
        pl.BlockSpec((tm, D_MODEL), lambda i, j: (i, 0), pipeline_mode=pl.Buffered(x_buffers)),
        vec,
        pl.BlockSpec((None, D_MODEL, tf), lambda i, j: (layer, 0, j)),
        pl.BlockSpec((None, D_MODEL, tf), lambda i, j: (layer, 0, j)),
        pl.BlockSpec((None, tf, D_MODEL), lambda i, j: (layer, j, 0)),
    ]
    args = [x, g, wg, wu, wd]
    out_specs = [tile]
    out_shape = [jax.ShapeDtypeStruct((rows, D_MODEL), F32)]
    if next_gain is not None:
        in_specs.append(vec)
        args.append(next_gain)
        out_specs.append(pl.BlockSpec((tm, D_MODEL), lambda i, j: (i, 0),
                                      pipeline_mode=pl.Buffered(1)))
        out_shape.append(jax.ShapeDtypeStruct((rows, D_MODEL), BF16))
    out = pl.pallas_call(
        _ffn_kernel,
        grid=(rows // tm, FFN_DIM // tf),
        in_specs=in_specs,
        out_specs=out_specs,
        out_shape=out_shape,
        scratch_shapes=[pltpu.VMEM((tm, D_MODEL), BF16)],
        compiler_params=pltpu.CompilerParams(
            dimension_semantics=("parallel", "arbitrary"),
            vmem_limit_bytes=FFN_VMEM_LIMIT),
        name="ffn",
    )(*args)
    return out if next_gain is not None else out[0]


def _inproj_kernel(h_ref, wt_ref, o_ref):
    col = pl.program_id(1) * PROJ_TILE + lax.broadcasted_iota(jnp.int32, (PROJ_TILE, 1), 0)
    wt = jnp.where(col < IN_COLS, wt_ref[...], 0.0).astype(BF16)
    o_ref[...] = lax.dot_general(h_ref[...], wt, (((1,), (1,)), ((), ())),
                                 preferred_element_type=F32)


def _inproj(layer, h, wt):
    rows = h.shape[0]
    tm = PROJ_ROWS
    return pl.pallas_call(
        _inproj_kernel,
        grid=(rows // tm, PROJ_COLS // PROJ_TILE),
        in_specs=[
            pl.BlockSpec((tm, D_MODEL), lambda i, j: (i, 0)),
            pl.BlockSpec((None, PROJ_TILE, D_MODEL), lambda i, j: (layer, j, 0)),
        ],
        out_specs=pl.BlockSpec((tm, PROJ_TILE), lambda i, j: (i, j)),
        out_shape=jax.ShapeDtypeStruct((rows, PROJ_COLS), F32),
        compiler_params=pltpu.CompilerParams(
            dimension_semantics=("parallel", "arbitrary"),
            vmem_limit_bytes=VMEM_LIMIT),
        name="inproj",
    )(h, wt)


def _gmlp_gate(gu, gv, norm_g):
    u = jax.nn.gelu(gu)
    zv = _rmsnorm(jax.nn.gelu(gv), norm_g)
    return u, zv


def _pool_gmlp_kernel(m_ref, pw_ref, ps_ref, gn_ref, ws_ref, gb_ref,
                      ya_ref, yb_ref, ext_ref):
    c = pl.program_id(1)
    hist = 16
    rows = POOL_ROWS

    @pl.when(c == 0)
    def _():
        ext_ref[0:hist, :] = jnp.zeros((hist, POOL_WIDTH), F32)

    a = m_ref[:, 0:POOL_WIDTH]
    ext_ref[hist:hist + rows, :] = a
    pos = c * rows + lax.broadcasted_iota(jnp.int32, (rows, 1), 0)
    for gi, w in enumerate(POOL_WINDOWS):
        sl = slice(gi * GROUP, (gi + 1) * GROUP)
        s = a[:, sl]
        for i in range(1, w):
            s = s + ext_ref[hist - i:hist - i + rows, sl]
        cnt = jnp.minimum(pos + 1, w).astype(F32)
        d = s / cnt - a[:, sl]
        y = jnp.dot(d.astype(BF16), pw_ref[gi], preferred_element_type=F32)
        ya_ref[:, sl] = y * ps_ref[:, sl]
    ext_ref[0:hist, :] = a[rows - hist:rows, :]

    u, zv = _gmlp_gate(m_ref[:, POOL_WIDTH:POOL_WIDTH + GMLP_WIDTH],
                       m_ref[:, POOL_WIDTH + GMLP_WIDTH:MIX_BLOCK], gn_ref[...])
    for ch in range(rows // GMLP_CHUNK):
        r = slice(ch * GMLP_CHUNK, (ch + 1) * GMLP_CHUNK)
        for gi in range(4):
            sl = slice(gi * GROUP, (gi + 1) * GROUP)
            s = jnp.dot(ws_ref[gi], zv[r, sl].astype(BF16), preferred_element_type=F32)
            yb_ref[r, sl] = u[r, sl] * (s + gb_ref[:, sl])


def _pool_gmlp(mix, pool_w, pool_scale, gmlp_norm, ws_tril, gb_full):
    chunks = SEQ // POOL_ROWS
    row = lambda b, c: b * chunks + c
    full = lambda shape: pl.BlockSpec(shape, lambda b, c: (0,) * len(shape))
    return pl.pallas_call(
        _pool_gmlp_kernel,
        grid=(BATCH, chunks),
        in_specs=[
            pl.BlockSpec((POOL_ROWS, MIX_BLOCK), lambda b, c: (row(b, c), 0)),
            full((4, GROUP, GROUP)),
            full((1, POOL_WIDTH)),
            full((1, GMLP_WIDTH)),
            full((4, GMLP_CHUNK, GMLP_CHUNK)),
            full((GMLP_CHUNK, GMLP_WIDTH)),
        ],
        out_specs=[
            pl.BlockSpec((POOL_ROWS, POOL_WIDTH), lambda b, c: (row(b, c), 0)),
            pl.BlockSpec((POOL_ROWS, GMLP_WIDTH), lambda b, c: (row(b, c), 0)),
        ],
        out_shape=[
            jax.ShapeDtypeStruct((N_ROWS, POOL_WIDTH), F32),
            jax.ShapeDtypeStruct((N_ROWS, GMLP_WIDTH), F32),
        ],
        scratch_shapes=[pltpu.VMEM((16 + POOL_ROWS, POOL_WIDTH), F32)],
        compiler_params=pltpu.CompilerParams(
            dimension_semantics=("parallel", "arbitrary"),
            vmem_limit_bytes=VMEM_LIMIT),
        name="pool_gmlp",
    )(mix, pool_w, pool_scale, gmlp_norm, ws_tril, gb_full)


def _pair_bd(y, left):
    zero = jnp.zeros_like(y)
    return jnp.concatenate([jnp.where(left, y, zero), jnp.where(left, zero, y)],
                           axis=1).astype(BF16)


def _pair_mm(x, y_bd):
    return _bdot(x, y_bd)


def _pair_inverse(m, left, blk, eye):
    d = jnp.where(blk, m, 0.0)
    low = m - d
    d2 = _pair_mm(d, _pair_bd(d, left))
    d2_bd = _pair_bd(d2, left)
    d4 = _pair_mm(d2, d2_bd)
    d4_bd = _pair_bd(d4, left)
    d8_bd = _pair_bd(_pair_mm(d4, d4_bd), left)
    td = eye - d
    td = td + _pair_mm(td, d2_bd)
    td = td + _pair_mm(td, d4_bd)
    td = td + _pair_mm(td, d8_bd)
    n = _pair_mm(td, _pair_bd(low, left))
    n2 = _pair_mm(n, _pair_bd(n, left))
    x = eye - n + n2 - _pair_mm(n, _pair_bd(n2, left))
    t0 = _pair_mm(x, _pair_bd(td, left))
    th, tl = _split(t0)
    mh, ml = _split(m)
    th_bd = _pair_bd(th, left)
    tl_bd = _pair_bd(tl, left)
    mt = _bdot(mh, th_bd) + (_bdot(mh, tl_bd) + _bdot(ml, th_bd))
    r = eye - t0 - mt
    return t0 + _bdot(th, _pair_bd(r, left))


def _cumsum_exact(g, tri, dims):
    g1 = g.astype(BF16)
    r1 = g - g1.astype(F32)
    g2 = r1.astype(BF16)
    g3 = (r1 - g2.astype(F32)).astype(BF16)
    if dims == 'rows':
        dot = lambda t: jnp.dot(tri, t, preferred_element_type=F32)
    else:
        dot = lambda t: jnp.dot(t, tri, preferred_element_type=F32)
    return dot(g1) + (dot(g2) + dot(g3))


def _delta_kernel(qk_ref, kv_ref, zb_ref, bat_ref, cw_ref, acol_ref, dtcol_ref,
                  arow_ref, dtrow_ref, on_ref, yc_ref, s_ref, ext_ref):
    n = pl.program_id(1)
    C = DN_CHUNK
    H = DN_HEADS
    R = DELTA_ROWS

    @pl.when(n == 0)
    def _():
        ext_ref[0:8, :] = jnp.zeros((8, CONV_DIM), F32)
        s_ref[...] = jnp.zeros(s_ref.shape, F32)

    ext_ref[8:8 + R, 0:MIX_BLOCK] = qk_ref[...]
    ext_ref[8:8 + R, MIX_BLOCK:CONV_DIM] = kv_ref[...]
    conv = cw_ref[3:4, :] * ext_ref[8:8 + R, :]
    for i in range(CONV_W - 1):
        conv = conv + cw_ref[i:i + 1, :] * ext_ref[5 + i:5 + i + R, :]
    ext_ref[0:8, :] = ext_ref[R:R + 8, :]
    act = _silu(conv)

    ri = lax.broadcasted_iota(jnp.int32, (C, 2 * C), 0)
    li = lax.broadcasted_iota(jnp.int32, (C, 2 * C), 1)
    ci = li % C
    left = li < C
    causal = ri >= ci
    strict = ri > ci
    blk = (ri // 16) == (ci // 16)
    eye = jnp.where(ri == ci, 1.0, 0.0).astype(F32)
    zero = jnp.zeros((C, DN_DK), F32)

    rr = lax.broadcasted_iota(jnp.int32, (R, R), 0)
    cc = lax.broadcasted_iota(jnp.int32, (R, R), 1)
    tril_chunks = jnp.where((rr // C == cc // C) & (rr >= cc), 1.0, 0.0).astype(BF16)
    pr = lax.broadcasted_iota(jnp.int32, (2 * C, 2 * C), 0)
    pc = lax.broadcasted_iota(jnp.int32, (2 * C, 2 * C), 1)
    triu_pair = jnp.where((pr // C == pc // C) & (pr <= pc), 1.0, 0.0).astype(BF16)

    ba = zb_ref[:, Z_WIDTH:Z_WIDTH + 128]
    beta_all = jax.nn.sigmoid(ba)
    g_all = -jnp.exp(acol_ref[...]) * _softplus(ba + dtcol_ref[...])
    gc_all = _cumsum_exact(g_all, tril_chunks, 'rows')
    on = on_ref[...]

    NC = R // C
    NP = H // 2
    lane2 = lambda a, b: jnp.concatenate([a, b], axis=1)
    diag2 = lambda a, b: jnp.concatenate([lane2(a, jnp.zeros_like(b)),
                                          lane2(jnp.zeros_like(a), b)], axis=0)

    ks, kbs, qes, gcs = [], [], [], []
    kq_lhs, kq_rhs, gc_pair, gr_pair, uw_rhs = [], [], [], [], []
    for c in range(NC):
        rows = slice(c * C, (c + 1) * C)
        heads = lambda base: jnp.stack(
            [act[rows, (base + h) * DN_DK:(base + h + 1) * DN_DK] for h in range(H)])
        q = heads(0)
        k = heads(H)
        v = heads(2 * H)
        q = q * lax.rsqrt(jnp.sum(q * q, axis=-1, keepdims=True) + EPS) * (DN_DK ** -0.5)
        k = k * lax.rsqrt(jnp.sum(k * k, axis=-1, keepdims=True) + EPS)
        beta = jnp.stack([beta_all[rows, h:h + 1] for h in range(H)])
        gc = jnp.stack([gc_all[rows, H + h:H + h + 1] for h in range(H)])
        egc = jnp.exp(gc)
        kb = k * beta
        vb = v * beta
        kbe = kb * egc
        ks.append(k)
        kbs.append(kb)
        qes.append(q * egc)
        gcs.append(gc)
        g_t = -jnp.exp(arow_ref[...]) * _softplus(bat_ref[c] + dtrow_ref[...])
        gr_all = _cumsum_exact(g_t, triu_pair, 'cols')
        for p in range(NP):
            a, b = 2 * p, 2 * p + 1
            kq_lhs.append(jnp.concatenate([lane2(kb[a], kb[b]), lane2(q[a], q[b])], axis=0))
            kq_rhs.append(diag2(k[a], k[b]))
            gc_pair.append(jnp.where(left, gc[a], gc[b]))
            gr_pair.append(gr_all[p:p + 1, :])
            uw_rhs.append(diag2(lane2(vb[a], kbe[a]), lane2(vb[b], kbe[b])))

    kq = jnp.einsum('bmk,bnk->bmn', jnp.stack(kq_lhs).astype(BF16), jnp.stack(kq_rhs).astype(BF16),
                    preferred_element_type=F32)
    diff = jnp.stack(gc_pair) - jnp.stack(gr_pair)
    decay = jnp.where(causal, jnp.exp(jnp.where(causal, diff, 0.0)), 0.0)
    m = jnp.where(strict, kq[:, 0:C] * decay, 0.0)
    a_mat = kq[:, C:2 * C] * decay
    t = _pair_inverse(m, left, blk, eye)
    uw = _bdot(t, jnp.stack(uw_rhs))

    for c in range(NC):
        rows = slice(c * C, (c + 1) * C)
        part = lambda h, i: uw[c * NP + h // 2][:, (2 * (h % 2) + i) * DN_DV:
                                               (2 * (h % 2) + i + 1) * DN_DV]
        u = jnp.stack([part(h, 0) for h in range(H)])
        w = jnp.stack([part(h, 1) for h in range(H)])
        s = s_ref[0]
        ws_qs = _bdot(jnp.concatenate([w, qes[c]], axis=1), s)
        v_new = u - ws_qs[:, 0:C]
        glast = gcs[c][:, C - 1:C, :]
        kg = ks[c] * jnp.exp(glast - gcs[c])
        s_ref[0] = s * jnp.exp(glast) + _bdot_tn(kg, v_new)
        vn_bd = jnp.stack([diag2(v_new[2 * p], v_new[2 * p + 1]) for p in range(NP)])
        av = _bdot(a_mat[c * NP:(c + 1) * NP], vn_bd)
        o = ws_qs[:, C:2 * C] + jnp.stack(
            [av[h // 2][:, (h % 2) * DN_DV:(h % 2 + 1) * DN_DV] for h in range(H)])
        o = o * lax.rsqrt(jnp.mean(o * o, axis=-1, keepdims=True) + EPS) * on
        for h in range(H):
            sl = slice(h * DN_DV, (h + 1) * DN_DV)
            yc_ref[rows, sl] = o[h] * _silu(zb_ref[rows, sl])


def _delta(mix, bat, conv_w, acol, dtcol, arow, dtrow, onorm):
    steps = SEQ // DELTA_ROWS
    cps = DELTA_ROWS // DN_CHUNK
    row = lambda b, n: b * steps + n
    full = lambda shape: pl.BlockSpec(shape, lambda b, n: (0,) * len(shape))
    return pl.pallas_call(
        _delta_kernel,
        grid=(BATCH, steps),
        in_specs=[
            pl.BlockSpec((DELTA_ROWS, MIX_BLOCK), lambda b, n: (row(b, n), 1)),
            pl.BlockSpec((DELTA_ROWS, MIX_BLOCK), lambda b, n: (row(b, n), 2)),
            pl.BlockSpec((DELTA_ROWS, MIX_BLOCK), lambda b, n: (row(b, n), 3)),
            pl.BlockSpec((cps, 8, 2 * DN_CHUNK), lambda b, n: (row(b, n), 0, 0)),
            full((CONV_W, CONV_DIM)),
            full((1, 128)),
            full((1, 128)),
            full((8, 2 * DN_CHUNK)),
            full((8, 2 * DN_CHUNK)),
            full((1, DN_DV)),
        ],
        out_specs=[
            pl.BlockSpec((DELTA_ROWS, Z_WIDTH), lambda b, n: (row(b, n), 0)),
            pl.BlockSpec((1, DN_HEADS, DN_DK, DN_DV), lambda b, n: (b, 0, 0, 0)),
        ],
        out_shape=[
            jax.ShapeDtypeStruct((N_PROMPT, Z_WIDTH), F32),
            jax.ShapeDtypeStruct((BATCH, DN_HEADS, DN_DK, DN_DV), F32),
        ],
        scratch_shapes=[pltpu.VMEM((8 + DELTA_ROWS, CONV_DIM), F32)],
        compiler_params=pltpu.CompilerParams(
            dimension_semantics=("parallel", "arbitrary"),
            vmem_limit_bytes=VMEM_LIMIT),
        name="delta",
    )(mix, mix, mix, bat, conv_w, acol, dtcol, arow, dtrow, onorm)


def _sample_rows_kernel(m_ref, qk_ref, kv_ref, zb_ref, pool_ref, cst_ref,
                        pw_ref, ps_ref, gn_ref, wsd_ref, gb0_ref, cw_ref, acol_ref, dtcol_ref,
                        ya_in, yb_in,
                        ya_ref, yb_ref, zv_ref, npool_ref, nconv_ref, vec_ref, eg_ref, qk_out):
    del ya_in, yb_in
    H = DN_HEADS
    a = m_ref[:, 0:POOL_WIDTH]
    for gi, w in enumerate(POOL_WINDOWS):
        sl = slice(gi * GROUP, (gi + 1) * GROUP)
        s = a[:, sl]
        for i in range(1, w):
            s = s + pool_ref[POOL_HIST - i, :, sl]
        d = s / float(w) - a[:, sl]
        y = jnp.dot(d.astype(BF16), pw_ref[gi], preferred_element_type=F32)
        ya_ref[:, sl] = y * ps_ref[:, sl]
    for i in range(POOL_HIST - 1):
        npool_ref[i] = pool_ref[i + 1]
    npool_ref[POOL_HIST - 1] = a

    u, zv = _gmlp_gate(m_ref[:, POOL_WIDTH:POOL_WIDTH + GMLP_WIDTH],
                       m_ref[:, POOL_WIDTH + GMLP_WIDTH:MIX_BLOCK], gn_ref[...])
    zv_ref[...] = zv
    yb_ref[...] = u * (wsd_ref[...] * zv + gb0_ref[...])

    x = jnp.concatenate([qk_ref[...], kv_ref[...]], axis=-1)
    conv = cw_ref[3:4, :] * x
    for i in range(CONV_W - 1):
        conv = conv + cw_ref[i:i + 1, :] * cst_ref[i]
    nconv_ref[0] = cst_ref[1]
    nconv_ref[1] = cst_ref[2]
    nconv_ref[2] = x
    act = _silu(conv)
    ba = zb_ref[:, Z_WIDTH:Z_WIDTH + 128]
    beta_all = jax.nn.sigmoid(ba)
    g_all = -jnp.exp(acol_ref[...]) * _softplus(ba + dtcol_ref[...])
    eg_all = jnp.exp(g_all)
    eg_ref[...] = eg_all
    lane = lax.broadcasted_iota(jnp.int32, (DEC_BATCH, 128), 1)
    qk_all = jnp.zeros((DEC_BATCH, 128), F32)
    for h in range(H):
        q = act[:, h * DN_DK:(h + 1) * DN_DK]
        k = act[:, (H + h) * DN_DK:(H + h + 1) * DN_DK]
        v = act[:, (2 * H + h) * DN_DK:(2 * H + h + 1) * DN_DK]
        q = q * lax.rsqrt(jnp.sum(q * q, axis=-1, keepdims=True) + EPS) * (DN_DK ** -0.5)
        k = k * lax.rsqrt(jnp.sum(k * k, axis=-1, keepdims=True) + EPS)
        beta = beta_all[:, h:h + 1]
        eg = eg_all[:, H + h:H + h + 1]
        vec_ref[:, h * 128:(h + 1) * 128] = k * (beta * eg)
        vec_ref[:, (H + h) * 128:(H + h + 1) * 128] = q * eg
        vec_ref[:, (2 * H + h) * 128:(2 * H + h + 1) * 128] = k
        vec_ref[:, (3 * H + h) * 128:(3 * H + h + 1) * 128] = v * beta
        qk = jnp.sum(q * k, axis=-1, keepdims=True)
        qk_all = jnp.where(lane == h, qk, qk_all)
    qk_out[...] = qk_all


def _sample_rows(mix, pool_t, conv_t, pool_w, pool_scale, gmlp_norm, ws_diag, gb0, conv_w,
                 acol, dtcol, ya, yb):
    rb = N_PROMPT // DEC_BATCH
    full = lambda shape: pl.BlockSpec(shape, lambda i: (0,) * len(shape))
    n = DEC_BATCH
    hbm = pl.BlockSpec(memory_space=pl.ANY)
    return pl.pallas_call(
        _sample_rows_kernel,
        grid=(1,),
        in_specs=[
            pl.BlockSpec((n, MIX_BLOCK), lambda i: (rb, 0)),
            pl.BlockSpec((n, MIX_BLOCK), lambda i: (rb, 1)),
            pl.BlockSpec((n, MIX_BLOCK), lambda i: (rb, 2)),
            pl.BlockSpec((n, MIX_BLOCK), lambda i: (rb, 3)),
            full((POOL_HIST, n, POOL_WIDTH)),
            full((CONV_W - 1, n, CONV_DIM)),
            full((4, GROUP, GROUP)),
            full((1, POOL_WIDTH)),
            full((1, GMLP_WIDTH)),
            full((1, GMLP_WIDTH)),
            full((1, GMLP_WIDTH)),
            full((CONV_W, CONV_DIM)),
            full((1, 128)),
            full((1, 128)),
            hbm,
            hbm,
        ],
        out_specs=[
            pl.BlockSpec((n, POOL_WIDTH), lambda i: (rb, 0)),
            pl.BlockSpec((n, GMLP_WIDTH), lambda i: (rb, 0)),
            full((n, GMLP_WIDTH)),
            full((POOL_HIST, n, POOL_WIDTH)),
            full((CONV_W - 1, n, CONV_DIM)),
            full((n, 4 * DN_HEADS * 128)),
            full((n, 128)),
            full((n, 128)),
        ],
        out_shape=[
            jax.ShapeDtypeStruct((N_ROWS, POOL_WIDTH), F32),
            jax.ShapeDtypeStruct((N_ROWS, GMLP_WIDTH), F32),
            jax.ShapeDtypeStruct((n, GMLP_WIDTH), F32),
            jax.ShapeDtypeStruct((POOL_HIST, n, POOL_WIDTH), F32),
            jax.ShapeDtypeStruct((CONV_W - 1, n, CONV_DIM), F32),
            jax.ShapeDtypeStruct((n, 4 * DN_HEADS * 128), F32),
            jax.ShapeDtypeStruct((n, 128), F32),
            jax.ShapeDtypeStruct((n, 128), F32),
        ],
        input_output_aliases={14: 0, 15: 1},
        compiler_params=pltpu.CompilerParams(
            dimension_semantics=("arbitrary",),
            vmem_limit_bytes=VMEM_LIMIT),
        name="sample_rows",
    )(mix, mix, mix, mix, pool_t, conv_t, pool_w, pool_scale, gmlp_norm,
      ws_diag, gb0, conv_w, acol, dtcol, ya, yb)


def _sample_state_kernel(vec_ref, scal_ref, z_ref, on_ref, s_ref, *rest):
    yc_ref, so_ref = rest[-2:]
    H = DN_HEADS
    row = lax.broadcasted_iota(jnp.int32, (1, 8, 1), 1)
    for h in range(H):
        s = s_ref[0, :, h]
        w = vec_ref[:, h:h + 1, :]
        qg = vec_ref[:, H + h:H + h + 1, :]
        k = vec_ref[:, 2 * H + h:2 * H + h + 1, :]
        vb = vec_ref[:, 3 * H + h:3 * H + h + 1, :]
        eg = scal_ref[:, h:h + 1, :]
        qk = scal_ref[:, H + h:H + h + 1, :]
        lhs = jnp.where(row == 0, w, jnp.where(row == 1, qg, 0.0))
        r = _bdot(lhs, s)
        v_new = vb - r[:, 0:1, :]
        o = r[:, 1:2, :] + qk * v_new
        k8 = jnp.where(row == 0, k, 0.0)
        v8 = jnp.broadcast_to(v_new, k8.shape)
        so_ref[0, :, h] = s * eg + _bdot_tn(k8, v8)
        o = o * lax.rsqrt(jnp.mean(o * o, axis=-1, keepdims=True) + EPS) * on_ref[...]
        yc_ref[:, h:h + 1, :] = o * _silu(z_ref[:, h:h + 1, :])


def _sample_state(layer, vec3, scal3, z3, onorm, state, stacked):
    sb = SAMPLE_BLOCK
    H = DN_HEADS
    in_specs = [
        pl.BlockSpec((sb, 4 * H, 128), lambda i: (i, 0, 0)),
        pl.BlockSpec((sb, 2 * H, 1), lambda i: (i, 0, 0)),
        pl.BlockSpec((sb, H, DN_DV), lambda i: (i, 0, 0)),
        pl.BlockSpec((1, DN_DV), lambda i: (0, 0)),
        pl.BlockSpec((1, sb, H, DN_DK, DN_DV), lambda i: (layer, i, 0, 0, 0)),
    ]
    args = [vec3, scal3, z3, onorm, state]
    aliases = {}
    if stacked is not None:
        in_specs.append(pl.BlockSpec(memory_space=pl.ANY))
        args.append(stacked)
        aliases = {5: 1}
    return pl.pallas_call(
        _sample_state_kernel,
        grid=(DEC_BATCH // sb,),
        in_specs=in_specs,
        out_specs=[
            pl.BlockSpec((sb, H, DN_DV), lambda i: (i, 0, 0)),
            pl.BlockSpec((1, sb, H, DN_DK, DN_DV), lambda i: (layer, i, 0, 0, 0)),
        ],
        out_shape=[
            jax.ShapeDtypeStruct((DEC_BATCH, H, DN_DV), F32),
            jax.ShapeDtypeStruct((DEPTH, DEC_BATCH, H, DN_DK, DN_DV), F32),
        ],
        input_output_aliases=aliases,
        compiler_params=pltpu.CompilerParams(
            dimension_semantics=("parallel",),
            vmem_limit_bytes=VMEM_LIMIT),
        name="sample_state",
    )(*args)


def _merge_kernel(x_ref, g0_ref, g1_ref, g2_ref, bg_ref, ya_ref, yb_ref, yc_ref,
                  pa_ref, pb_ref, pc_ref, wo_ref, *rest):
    o_ref = rest[-1]
    lane = lax.broadcasted_iota(jnp.int32, (1, GATE_WIN), 1)
    valid = (lane >= GATE_SHIFT) & (lane < GATE_SHIFT + D_MODEL)

    def branch(g_ref, i, y_ref, p_ref):
        gate = jax.nn.sigmoid(g_ref[...] + bg_ref[:, i * GATE_WIN:(i + 1) * GATE_WIN])
        gate = jnp.where(valid, gate, 0.0)
        return gate * jnp.dot(y_ref[...].astype(BF16), p_ref[...], preferred_element_type=F32)

    m = branch(g0_ref, 0, ya_ref, pa_ref)
    m = m + branch(g1_ref, 1, yb_ref, pb_ref)
    m = m + branch(g2_ref, 2, yc_ref, pc_ref)
    o_ref[...] = x_ref[...] + jnp.dot(m.astype(BF16), wo_ref[...], preferred_element_type=F32)


def _merge(x, proj, b_gate, ya, yb, yc, pa, pb, pc, wo, tile, first, count, prev):
    once = lambda shape: pl.BlockSpec(shape, lambda i: (0,) * len(shape),
                                      pipeline_mode=pl.Buffered(1))
    rows = lambda width: pl.BlockSpec((tile, width), lambda i: (first + i, 0))
    gate = lambda k: pl.BlockSpec((pl.Element(tile), pl.Element(GATE_WIN)),
                                  lambda i: ((first + i) * tile, COL_BA + k * D_MODEL))
    in_specs = [
        rows(D_MODEL),
        gate(0), gate(1), gate(2),
        once((1, 3 * GATE_WIN)),
        rows(POOL_WIDTH), rows(GMLP_WIDTH),
        pl.BlockSpec((tile, Z_WIDTH), lambda i: (i, 0)),
        once((POOL_WIDTH, GATE_WIN)),
        once((GMLP_WIDTH, GATE_WIN)),
        once((Z_WIDTH, GATE_WIN)),
        once((GATE_WIN, D_MODEL)),
    ]
    args = [x, proj, proj, proj, b_gate, ya, yb, yc, pa, pb, pc, wo]
    aliases = {}
    if prev is not None:
        in_specs.append(pl.BlockSpec(memory_space=pl.ANY))
        args.append(prev)
        aliases = {len(args) - 1: 0}
    return pl.pallas_call(
        _merge_kernel,
        grid=(count,),
        in_specs=in_specs,
        out_specs=rows(D_MODEL),
        out_shape=jax.ShapeDtypeStruct((N_ROWS, D_MODEL), F32),
        input_output_aliases=aliases,
        compiler_params=pltpu.CompilerParams(
            dimension_semantics=("parallel",),
            vmem_limit_bytes=VMEM_LIMIT),
        name="merge",
    )(*args)


def _final_kernel(x_ref, g_ref, o_ref):
    o_ref[...] = _rmsnorm(x_ref[...], g_ref[...])


def _final_norm(x, g, t, first, count):
    return pl.pallas_call(
        _final_kernel,
        grid=(count,),
        in_specs=[pl.BlockSpec((t, D_MODEL), lambda i: (first + i, 0)),
                  pl.BlockSpec((1, D_MODEL), lambda i: (0, 0))],
        out_specs=pl.BlockSpec((t, D_MODEL), lambda i: (i, 0)),
        out_shape=jax.ShapeDtypeStruct((count * t, D_MODEL), F32),
        compiler_params=pltpu.CompilerParams(dimension_semantics=("parallel",),
                                             vmem_limit_bytes=VMEM_LIMIT),
        name="final_norm",
    )(x, g)


def _lane_pad(v, offset, width=128):
    return jnp.zeros((1, width), F32).at[0, offset:offset + v.shape[0]].set(v)


def _tail_rows(mix, count, col, width):
    return jnp.stack([
        lax.slice(mix, ((b + 1) * SEQ - count, col), ((b + 1) * SEQ, col + width))
        for b in range(BATCH)])


def kernel(x_prompt, x_sample, state_delta, state_conv, state_pool, ffn1_norm, ffn1_wg, ffn1_wu, ffn1_wd, mix_norm, w_in, b_gate, pool_w, pool_scale, gmlp_norm, gmlp_ws, gmlp_b, dn_conv, dn_a_log, dn_dt_bias, dn_onorm, proj_a, proj_b, proj_c, w_o, ffn2_norm, ffn2_wg, ffn2_wu, ffn2_wd, final_norm):
    H = DN_HEADS
    x = jnp.concatenate([x_prompt.reshape(N_PROMPT, D_MODEL),
                         x_sample.reshape(DEC_BATCH, D_MODEL)], axis=0)
    causal = jnp.tril(jnp.ones((GMLP_CHUNK, GMLP_CHUNK), bool))
    outs_delta_p, outs_conv_p, outs_conv_s = [], [], []
    outs_pool_p, outs_pool_s, outs_zv = [], [], []
    delta_s = None
    w_in_t = jnp.swapaxes(w_in, 1, 2)
    shift_cols = ((0, 0), (GATE_SHIFT, GATE_WIN - D_MODEL - GATE_SHIFT))
    for l in range(DEPTH):
        x, h_mix = _ffn(l, x, ffn1_norm[l][None], ffn1_wg, ffn1_wu, ffn1_wd,
                        next_gain=mix_norm[l][None])

        mix = _inproj(l, h_mix, w_in_t)

        pw = pool_w[l].astype(BF16)
        ps = pool_scale[l][None]
        gn = gmlp_norm[l][None]
        ws_tril = jnp.where(causal, gmlp_ws[l], 0).astype(BF16)
        gb_full = jnp.repeat(gmlp_b[l].T, GROUP, axis=1)
        ws_diag = jnp.repeat(gmlp_ws[l][:, 0, 0], GROUP)[None]
        gb0 = jnp.repeat(gmlp_b[l][:, 0], GROUP)[None]
        acol = _lane_pad(dn_a_log[l], H)
        dtcol = _lane_pad(dn_dt_bias[l], H)
        pair_rows = lambda v: jnp.pad(jnp.repeat(v.reshape(H // 2, 2), DN_CHUNK, axis=1),
                                      ((0, 8 - H // 2), (0, 0)))
        arow = pair_rows(dn_a_log[l])
        dtrow = pair_rows(dn_dt_bias[l])
        onorm = dn_onorm[l][None]

        ya, yb = _pool_gmlp(mix, pw, ps, gn, ws_tril, gb_full)
        a_p = lax.slice(mix, (0, COL_BA + H), (N_PROMPT, COL_BA + 2 * H))
        bat = a_p.reshape(N_PROMPT // DN_CHUNK, DN_CHUNK, H // 2, 2).transpose(0, 2, 3, 1)
        bat = jnp.pad(bat.reshape(N_PROMPT // DN_CHUNK, H // 2, 2 * DN_CHUNK),
                      ((0, 0), (0, 8 - H // 2), (0, 0)))
        yc_p, s_p = _delta(mix, bat, dn_conv[l], acol, dtcol, arow, dtrow, onorm)
        outs_delta_p.append(s_p)
        outs_conv_p.append(_tail_rows(mix, CONV_W - 1, COL_QKV, CONV_DIM))
        outs_pool_p.append(_tail_rows(mix, POOL_HIST, 0, POOL_WIDTH))

        pool_t = state_pool[l].transpose(1, 0, 2)
        conv_t = state_conv[l].transpose(1, 0, 2)
        ya, yb, zv_s, npool_t, nconv_t, vec, eg_all, qk_all = _sample_rows(
            mix, pool_t, conv_t, pw, ps, gn, ws_diag, gb0, dn_conv[l], acol, dtcol, ya, yb)
        vec3 = vec.reshape(DEC_BATCH, 4 * H, 128)
        scal3 = jnp.concatenate([eg_all[:, H:2 * H], qk_all[:, :H]], axis=1)[:, :, None]
        z3 = lax.slice(mix, (N_PROMPT, COL_Z), (N_ROWS, COL_Z + Z_WIDTH)).reshape(DEC_BATCH, H, DN_DV)
        yc_s3, delta_s = _sample_state(l, vec3, scal3, z3, onorm, state_delta, delta_s)
        outs_conv_s.append(nconv_t.transpose(1, 0, 2))
        outs_pool_s.append(npool_t.transpose(1, 0, 2))
        outs_zv.append(zv_s[:, None, :])

        weights = (jnp.pad(proj_a[l].astype(BF16), shift_cols),
                   jnp.pad(proj_b[l].astype(BF16), shift_cols),
                   jnp.pad(proj_c[l].astype(BF16), shift_cols),
                   jnp.pad(w_o[l].astype(BF16), shift_cols[::-1]))
        bg = jnp.pad(b_gate[l].reshape(3, D_MODEL), shift_cols).reshape(1, 3 * GATE_WIN)
        x_new = _merge(x, mix, bg, ya, yb, yc_p, *weights,
                       tile=MERGE_TILE, first=0, count=N_PROMPT // MERGE_TILE, prev=None)
        x = _merge(x, mix, bg, ya, yb, yc_s3.reshape(DEC_BATCH, Z_WIDTH), *weights,
                   tile=DEC_BATCH, first=N_PROMPT // DEC_BATCH, count=1, prev=x_new)

        x = _ffn(l, x, ffn2_norm[l][None], ffn2_wg, ffn2_wu, ffn2_wd)

    g = final_norm[None]
    y_prompt = _final_norm(x, g, NORM_TILE, 0, N_PROMPT // NORM_TILE).reshape(BATCH, SEQ, D_MODEL)
    y_sample = _final_norm(x, g, DEC_BATCH, N_PROMPT // DEC_BATCH, 1).reshape(
        DEC_BATCH, 1, D_MODEL)
    return (y_prompt, y_sample, jnp.stack(outs_delta_p), delta_s,
            jnp.stack(outs_conv_p), jnp.stack(outs_conv_s), jnp.stack(outs_pool_p),
            jnp.stack(outs_pool_s), jnp.stack(outs_zv))
```

```python
import jax
import jax.numpy as jnp
from jax import lax
from jax.experimental import pallas as pl
from jax.experimental.pallas import tpu as pltpu

F32 = jnp.float32
BF16 = jnp.bfloat16

D_MODEL = 2048
BATCH = 4
SEQ = 2048
DEPTH = 4
DEC_BATCH = 128
EPS = 1e-6
FFN_DIM = 5632
POOL_WINDOWS = (2, 4, 8, 16)
POOL_WIDTH = 512
POOL_HIST = 15
GMLP_CHUNK = 128
GMLP_WIDTH = 512
DN_HEADS = 8
DN_DK = 128
DN_DV = 128
DN_CHUNK = 64
CONV_W = 4
CONV_DIM = 3072
GROUP = 128

N_PROMPT = BATCH * SEQ
N_ROWS = N_PROMPT + DEC_BATCH

COL_QKV = 1536
COL_Z = 4608
COL_BA = 5632
COL_GATE = 5648
IN_COLS = 11792
PROJ_COLS = 12288
MIX_BLOCK = 1536
Z_WIDTH = DN_HEADS * DN_DV
GATE_SHIFT = COL_GATE - COL_BA
GATE_WIN = D_MODEL + 128

ROW_TILE = 1040
MERGE_TILE = 256
FFN_TILE = 256
PROJ_ROWS = 2080
PROJ_TILE = 512
NORM_TILE = 512
DELTA_ROWS = 512
POOL_ROWS = 512
SAMPLE_BLOCK = 8
VMEM_LIMIT = 56 * 1024 * 1024
FFN_VMEM_LIMIT = 60 * 1024 * 1024


def _rmsnorm(x, g):
    ms = jnp.mean(x * x, axis=-1, keepdims=True)
    return x * lax.rsqrt(ms + EPS) * g


def _silu(x):
    return x * jax.nn.sigmoid(x)


def _softplus(x):
    return jnp.maximum(x, 0.0) + jnp.log1p(jnp.exp(-jnp.abs(x)))


def _bdot(a, b):
    return jnp.einsum('hmk,hkn->hmn', a.astype(BF16), b.astype(BF16),
                      preferred_element_type=F32)


def _bdot_tn(a, b):
    return jnp.einsum('hcm,hcn->hmn', a.astype(BF16), b.astype(BF16),
                      preferred_element_type=F32)


def _split(x):
    hi = x.astype(BF16)
    lo = (x - hi.astype(F32)).astype(BF16)
    return hi, lo


def _ffn_kernel(x_ref, g_ref, wg_ref, wu_ref, wd_ref, *rest):
    emit_norm = len(rest) == 4
    if emit_norm:
        gn_ref, o_ref, n_ref, h_ref = rest
    else:
        o_ref, h_ref = rest
    j = pl.program_id(1)

    @pl.when(j == 0)
    def _():
        x = x_ref[...]
        h_ref[...] = _rmsnorm(x, g_ref[...]).astype(BF16)
        o_ref[...] = x

    h = h_ref[...]
    a = jnp.dot(h, wg_ref[...].astype(BF16), preferred_element_type=F32)
    b = jnp.dot(h, wu_ref[...].astype(BF16), preferred_element_type=F32)
    act = (_silu(a) * (0.5 * b)).astype(BF16)
    o_ref[...] += jnp.dot(act, wd_ref[...].astype(BF16), preferred_element_type=F32)

    if emit_norm:
        @pl.when(j == pl.num_programs(1) - 1)
        def _():
            n_ref[...] = _rmsnorm(o_ref[...], gn_ref[...]).astype(BF16)


def _ffn(layer, x, g, wg, wu, wd, next_gain=None, tm=ROW_TILE, tf=FFN_TILE, x_buffers=2):
    rows = x.shape[0]
    vec = pl.BlockSpec((1, D_MODEL), lambda i, j: (0, 0))
    tile = pl.BlockSpec((tm, D_MODEL), lambda i, j: (i, 0))
    in_specs = [
        pl.BlockSpec((tm, D_MODEL), lambda i, j: (i, 0), pipeline_mode=pl.Buffered(x_buffers)),
        vec,
        pl.BlockSpec((None, D_MODEL, tf), lambda i, j: (layer, 0, j)),
        pl.BlockSpec((None, D_MODEL, tf), lambda i, j: (layer, 0, j)),
        pl.BlockSpec((None, tf, D_MODEL), lambda i, j: (layer, j, 0)),
    ]
    args = [x, g, wg, wu, wd]
    out_specs = [tile]
    out_shape = [jax.ShapeDtypeStruct((rows, D_MODEL), F32)]
    if next_gain is not None:
        in_specs.append(vec)
        args.append(next_gain)
        out_specs.append(pl.BlockSpec((tm, D_MODEL), lambda i, j: (i, 0),
                                      pipeline_mode=pl.Buffered(1)))
        out_shape.append(jax.ShapeDtypeStruct((rows, D_MODEL), BF16))
    out = pl.pallas_call(
        _ffn_kernel,
        grid=(rows // tm, FFN_DIM // tf),
        in_specs=in_specs,
        out_specs=out_specs,
        out_shape=out_shape,
        scratch_shapes=[pltpu.VMEM((tm, D_MODEL), BF16)],
        compiler_params=pltpu.CompilerParams(
            dimension_semantics=("parallel", "arbitrary"),
            vmem_limit_bytes=FFN_VMEM_LIMIT),
        name="ffn",
    )(*args)
    return out if next_gain is not None else out[0]


def _inproj_kernel(h_ref, wt_ref, o_ref):
    col = pl.program_id(1) * PROJ_TILE + lax.broadcasted_iota(jnp.int32, (PROJ_TILE, 1), 0)
    wt = jnp.where(col < IN_COLS, wt_ref[...], 0.0).astype(BF16)
    o_ref[...] = lax.dot_general(h_ref[...], wt, (((1,), (1,)), ((), ())),
                                 preferred_element_type=F32)


def _inproj(layer, h, wt):
    rows = h.shape[0]
    tm = PROJ_ROWS
    return pl.pallas_call(
        _inproj_kernel,
        grid=(rows // tm, PROJ_COLS // PROJ_TILE),
        in_specs=[
            pl.BlockSpec((tm, D_MODEL), lambda i, j: (i, 0)),
            pl.BlockSpec((None, PROJ_TILE, D_MODEL), lambda i, j: (layer, j, 0)),
        ],
        out_specs=pl.BlockSpec((tm, PROJ_TILE), lambda i, j: (i, j)),
        out_shape=jax.ShapeDtypeStruct((rows, PROJ_COLS), F32),
        compiler_params=pltpu.CompilerParams(
            dimension_semantics=("parallel", "arbitrary"),
            vmem_limit_bytes=VMEM_LIMIT),
        name="inproj",
    )(h, wt)


def _gmlp_gate(gu, gv, norm_g):
    u = jax.nn.gelu(gu)
    zv = _rmsnorm(jax.nn.gelu(gv), norm_g)
    return u, zv


def _pool_gmlp_kernel(m_ref, pw_ref, ps_ref, gn_ref, ws_ref, gb_ref,
                      ya_ref, yb_ref, ext_ref):
    c = pl.program_id(1)
    hist = 16
    rows = POOL_ROWS

    @pl.when(c == 0)
    def _():
        ext_ref[0:hist, :] = jnp.zeros((hist, POOL_WIDTH), F32)

    a = m_ref[:, 0:POOL_WIDTH]
    ext_ref[hist:hist + rows, :] = a
    pos = c * rows + lax.broadcasted_iota(jnp.int32, (rows, 1), 0)
    for gi, w in enumerate(POOL_WINDOWS):
        sl = slice(gi * GROUP, (gi + 1) * GROUP)
        s = a[:, sl]
        for i in range(1, w):
            s = s + ext_ref[hist - i:hist - i + rows, sl]
        cnt = jnp.minimum(pos + 1, w).astype(F32)
        d = s / cnt - a[:, sl]
        y = jnp.dot(d.astype(BF16), pw_ref[gi], preferred_element_type=F32)
        ya_ref[:, sl] = y * ps_ref[:, sl]
    ext_ref[0:hist, :] = a[rows - hist:rows, :]

    u, zv = _gmlp_gate(m_ref[:, POOL_WIDTH:POOL_WIDTH + GMLP_WIDTH],
                       m_ref[:, POOL_WIDTH + GMLP_WIDTH:MIX_BLOCK], gn_ref[...])
    for ch in range(rows // GMLP_CHUNK):
        r = slice(ch * GMLP_CHUNK, (ch + 1) * GMLP_CHUNK)
        for gi in range(4):
            sl = slice(gi * GROUP, (gi + 1) * GROUP)
            s = jnp.dot(ws_ref[gi], zv[r, sl].astype(BF16), preferred_element_type=F32)
            yb_ref[r, sl] = u[r, sl] * (s + gb_ref[:, sl])


def _pool_gmlp(mix, pool_w, pool_scale, gmlp_norm, ws_tril, gb_full):
    chunks = SEQ // POOL_ROWS
    row = lambda b, c: b * chunks + c
    full = lambda shape: pl.BlockSpec(shape, lambda b, c: (0,) * len(shape))
    return pl.pallas_call(
        _pool_gmlp_kernel,
        grid=(BATCH, chunks),
        in_specs=[
            pl.BlockSpec((POOL_ROWS, MIX_BLOCK), lambda b, c: (row(b, c), 0)),
            full((4, GROUP, GROUP)),
            full((1, POOL_WIDTH)),
            full((1, GMLP_WIDTH)),
            full((4, GMLP_CHUNK, GMLP_CHUNK)),
            full((GMLP_CHUNK, GMLP_WIDTH)),
        ],
        out_specs=[
            pl.BlockSpec((POOL_ROWS, POOL_WIDTH), lambda b, c: (row(b, c), 0)),
            pl.BlockSpec((POOL_ROWS, GMLP_WIDTH), lambda b, c: (row(b, c), 0)),
        ],
        out_shape=[
            jax.ShapeDtypeStruct((N_ROWS, POOL_WIDTH), F32),
            jax.ShapeDtypeStruct((N_ROWS, GMLP_WIDTH), F32),
        ],
        scratch_shapes=[pltpu.VMEM((16 + POOL_ROWS, POOL_WIDTH), F32)],
        compiler_params=pltpu.CompilerParams(
            dimension_semantics=("parallel", "arbitrary"),
            vmem_limit_bytes=VMEM_LIMIT),
        name="pool_gmlp",
    )(mix, pool_w, pool_scale, gmlp_norm, ws_tril, gb_full)


def _pair_bd(y, left):
    zero = jnp.zeros_like(y)
    return jnp.concatenate([jnp.where(left, y, zero), jnp.where(left, zero, y)],
                           axis=1).astype(BF16)


def _pair_mm(x, y_bd):
    return _bdot(x, y_bd)


def _pair_inverse(m, left, blk, eye):
    d = jnp.where(blk, m, 0.0)
    low = m - d
    d2 = _pair_mm(d, _pair_bd(d, left))
    d2_bd = _pair_bd(d2, left)
    d4 = _pair_mm(d2, d2_bd)
    d4_bd = _pair_bd(d4, left)
    d8_bd = _pair_bd(_pair_mm(d4, d4_bd), left)
    td = eye - d
    td = td + _pair_mm(td, d2_bd)
    td = td + _pair_mm(td, d4_bd)
    td = td + _pair_mm(td, d8_bd)
    n = _pair_mm(td, _pair_bd(low, left))
    n2 = _pair_mm(n, _pair_bd(n, left))
    x = eye - n + n2 - _pair_mm(n, _pair_bd(n2, left))
    t0 = _pair_mm(x, _pair_bd(td, left))
    th, tl = _split(t0)
    mh, ml = _split(m)
    th_bd = _pair_bd(th, left)
    tl_bd = _pair_bd(tl, left)
    mt = _bdot(mh, th_bd) + (_bdot(mh, tl_bd) + _bdot(ml, th_bd))
    r = eye - t0 - mt
    return t0 + _bdot(th, _pair_bd(r, left))


def _cumsum_exact(g, tri, dims):
    g1 = g.astype(BF16)
    r1 = g - g1.astype(F32)
    g2 = r1.astype(BF16)
    g3 = (r1 - g2.astype(F32)).astype(BF16)
    if dims == 'rows':
        dot = lambda t: jnp.dot(tri, t, preferred_element_type=F32)
    else:
        dot = lambda t: jnp.dot(t, tri, preferred_element_type=F32)
    return dot(g1) + (dot(g2) + dot(g3))


def _delta_kernel(qk_ref, kv_ref, zb_ref, bat_ref, cw_ref, acol_ref, dtcol_ref,
                  arow_ref, dtrow_ref, on_ref, yc_ref, s_ref, ext_ref):
    n = pl.program_id(1)
    C = DN_CHUNK
    H = DN_HEADS
    R = DELTA_ROWS

    @pl.when(n == 0)
    def _():
        ext_ref[0:8, :] = jnp.zeros((8, CONV_DIM), F32)
        s_ref[...] = jnp.zeros(s_ref.shape, F32)

    ext_ref[8:8 + R, 0:MIX_BLOCK] = qk_ref[...]
    ext_ref[8:8 + R, MIX_BLOCK:CONV_DIM] = kv_ref[...]
    conv = cw_ref[3:4, :] * ext_ref[8:8 + R, :]
    for i in range(CONV_W - 1):
        conv = conv + cw_ref[i:i + 1, :] * ext_ref[5 + i:5 + i + R, :]
    ext_ref[0:8, :] = ext_ref[R:R + 8, :]
    act = _silu(conv)

    ri = lax.broadcasted_iota(jnp.int32, (C, 2 * C), 0)
    li = lax.broadcasted_iota(jnp.int32, (C, 2 * C), 1)
    ci = li % C
    left = li < C
    causal = ri >= ci
    strict = ri > ci
    blk = (ri // 16) == (ci // 16)
    eye = jnp.where(ri == ci, 1.0, 0.0).astype(F32)
    zero = jnp.zeros((C, DN_DK), F32)

    rr = lax.broadcasted_iota(jnp.int32, (R, R), 0)
    cc = lax.broadcasted_iota(jnp.int32, (R, R), 1)
    tril_chunks = jnp.where((rr // C == cc // C) & (rr >= cc), 1.0, 0.0).astype(BF16)
    pr = lax.broadcasted_iota(jnp.int32, (2 * C, 2 * C), 0)
    pc = lax.broadcasted_iota(jnp.int32, (2 * C, 2 * C), 1)
    triu_pair = jnp.where((pr // C == pc // C) & (pr <= pc), 1.0, 0.0).astype(BF16)

    ba = zb_ref[:, Z_WIDTH:Z_WIDTH + 128]
    beta_all = jax.nn.sigmoid(ba)
    g_all = -jnp.exp(acol_ref[...]) * _softplus(ba + dtcol_ref[...])
    gc_all = _cumsum_exact(g_all, tril_chunks, 'rows')
    on = on_ref[...]

    NC = R // C
    NP = H // 2
    lane2 = lambda a, b: jnp.concatenate([a, b], axis=1)
    diag2 = lambda a, b: jnp.concatenate([lane2(a, jnp.zeros_like(b)),
                                          lane2(jnp.zeros_like(a), b)], axis=0)

    ks, kbs, qes, gcs = [], [], [], []
    kq_lhs, kq_rhs, gc_pair, gr_pair, uw_rhs = [], [], [], [], []
    for c in range(NC):
        rows = slice(c * C, (c + 1) * C)
        heads = lambda base: jnp.stack(
            [act[rows, (base + h) * DN_DK:(base + h + 1) * DN_DK] for h in range(H)])
        q = heads(0)
        k = heads(H)
        v = heads(2 * H)
        q = q * lax.rsqrt(jnp.sum(q * q, axis=-1, keepdims=True) + EPS) * (DN_DK ** -0.5)
        k = k * lax.rsqrt(jnp.sum(k * k, axis=-1, keepdims=True) + EPS)
        beta = jnp.stack([beta_all[rows, h:h + 1] for h in range(H)])
        gc = jnp.stack([gc_all[rows, H + h:H + h + 1] for h in range(H)])
        egc = jnp.exp(gc)
        kb = k * beta
        vb = v * beta
        kbe = kb * egc
        ks.append(k)
        kbs.append(kb)
        qes.append(q * egc)
        gcs.append(gc)
        g_t = -jnp.exp(arow_ref[...]) * _softplus(bat_ref[c] + dtrow_ref[...])
        gr_all = _cumsum_exact(g_t, triu_pair, 'cols')
        for p in range(NP):
            a, b = 2 * p, 2 * p + 1
            kq_lhs.append(jnp.concatenate([lane2(kb[a], kb[b]), lane2(q[a], q[b])], axis=0))
            kq_rhs.append(diag2(k[a], k[b]))
            gc_pair.append(jnp.where(left, gc[a], gc[b]))
            gr_pair.append(gr_all[p:p + 1, :])
            uw_rhs.append(diag2(lane2(vb[a], kbe[a]), lane2(vb[b], kbe[b])))

    kq = jnp.einsum('bmk,bnk->bmn', jnp.stack(kq_lhs).astype(BF16), jnp.stack(kq_rhs).astype(BF16),
                    preferred_element_type=F32)
    diff = jnp.stack(gc_pair) - jnp.stack(gr_pair)
    decay = jnp.where(causal, jnp.exp(jnp.where(causal, diff, 0.0)), 0.0)
    m = jnp.where(strict, kq[:, 0:C] * decay, 0.0)
    a_mat = kq[:, C:2 * C] * decay
    t = _pair_inverse(m, left, blk, eye)
    uw = _bdot(t, jnp.stack(uw_rhs))

    for c in range(NC):
        rows = slice(c * C, (c + 1) * C)
        part = lambda h, i: uw[c * NP + h // 2][:, (2 * (h % 2) + i) * DN_DV:
                                               (2 * (h % 2) + i + 1) * DN_DV]
        u = jnp.stack([part(h, 0) for h in range(H)])
        w = jnp.stack([part(h, 1) for h in range(H)])
        s = s_ref[0]
        ws_qs = _bdot(jnp.concatenate([w, qes[c]], axis=1), s)
        v_new = u - ws_qs[:, 0:C]
        glast = gcs[c][:, C - 1:C, :]
        kg = ks[c] * jnp.exp(glast - gcs[c])
        s_ref[0] = s * jnp.exp(glast) + _bdot_tn(kg, v_new)
        vn_bd = jnp.stack([diag2(v_new[2 * p], v_new[2 * p + 1]) for p in range(NP)])
        av = _bdot(a_mat[c * NP:(c + 1) * NP], vn_bd)
        o = ws_qs[:, C:2 * C] + jnp.stack(
            [av[h // 2][:, (h % 2) * DN_DV:(h % 2 + 1) * DN_DV] for h in range(H)])
        o = o * lax.rsqrt(jnp.mean(o * o, axis=-1, keepdims=True) + EPS) * on
        for h in range(H):
            sl = slice(h * DN_DV, (h + 1) * DN_DV)
            yc_ref[rows, sl] = o[h] * _silu(zb_ref[rows, sl])


def _delta(mix, bat, conv_w, acol, dtcol, arow, dtrow, onorm):
    steps = SEQ // DELTA_ROWS
    cps = DELTA_ROWS // DN_CHUNK
    row = lambda b, n: b * steps + n
    full = lambda shape: pl.BlockSpec(shape, lambda b, n: (0,) * len(shape))
    return pl.pallas_call(
        _delta_kernel,
        grid=(BATCH, steps),
        in_specs=[
            pl.BlockSpec((DELTA_ROWS, MIX_BLOCK), lambda b, n: (row(b, n), 1)),
            pl.BlockSpec((DELTA_ROWS, MIX_BLOCK), lambda b, n: (row(b, n), 2)),
            pl.BlockSpec((DELTA_ROWS, MIX_BLOCK), lambda b, n: (row(b, n), 3)),
            pl.BlockSpec((cps, 8, 2 * DN_CHUNK), lambda b, n: (row(b, n), 0, 0)),
            full((CONV_W, CONV_DIM)),
            full((1, 128)),
            full((1, 128)),
            full((8, 2 * DN_CHUNK)),
            full((8, 2 * DN_CHUNK)),
            full((1, DN_DV)),
        ],
        out_specs=[
            pl.BlockSpec((DELTA_ROWS, Z_WIDTH), lambda b, n: (row(b, n), 0)),
            pl.BlockSpec((1, DN_HEADS, DN_DK, DN_DV), lambda b, n: (b, 0, 0, 0)),
        ],
        out_shape=[
            jax.ShapeDtypeStruct((N_PROMPT, Z_WIDTH), F32),
            jax.ShapeDtypeStruct((BATCH, DN_HEADS, DN_DK, DN_DV), F32),
        ],
        scratch_shapes=[pltpu.VMEM((8 + DELTA_ROWS, CONV_DIM), F32)],
        compiler_params=pltpu.CompilerParams(
            dimension_semantics=("parallel", "arbitrary"),
            vmem_limit_bytes=VMEM_LIMIT),
        name="delta",
    )(mix, mix, mix, bat, conv_w, acol, dtcol, arow, dtrow, onorm)


def _sample_rows_kernel(m_ref, qk_ref, kv_ref, zb_ref, pool_ref, cst_ref,
                        pw_ref, ps_ref, gn_ref, wsd_ref, gb0_ref, cw_ref, acol_ref, dtcol_ref,
                        ya_in, yb_in,
                        ya_ref, yb_ref, zv_ref, npool_ref, nconv_ref, vec_ref, eg_ref, qk_out):
    del ya_in, yb_in
    H = DN_HEADS
    a = m_ref[:, 0:POOL_WIDTH]
    for gi, w in enumerate(POOL_WINDOWS):
        sl = slice(gi * GROUP, (gi + 1) * GROUP)
        s = a[:, sl]
        for i in range(1, w):
            s = s + pool_ref[POOL_HIST - i, :, sl]
        d = s / float(w) - a[:, sl]
        y = jnp.dot(d.astype(BF16), pw_ref[gi], preferred_element_type=F32)
        ya_ref[:, sl] = y * ps_ref[:, sl]
    for i in range(POOL_HIST - 1):
        npool_ref[i] = pool_ref[i + 1]
    npool_ref[POOL_HIST - 1] = a

    u, zv = _gmlp_gate(m_ref[:, POOL_WIDTH:POOL_WIDTH + GMLP_WIDTH],
                       m_ref[:, POOL_WIDTH + GMLP_WIDTH:MIX_BLOCK], gn_ref[...])
    zv_ref[...] = zv
    yb_ref[...] = u * (wsd_ref[...] * zv + gb0_ref[...])

    x = jnp.concatenate([qk_ref[...], kv_ref[...]], axis=-1)
    conv = cw_ref[3:4, :] * x
    for i in range(CONV_W - 1):
        conv = conv + cw_ref[i:i + 1, :] * cst_ref[i]
    nconv_ref[0] = cst_ref[1]
    nconv_ref[1] = cst_ref[2]
    nconv_ref[2] = x
    act = _silu(conv)
    ba = zb_ref[:, Z_WIDTH:Z_WIDTH + 128]
    beta_all = jax.nn.sigmoid(ba)
    g_all = -jnp.exp(acol_ref[...]) * _softplus(ba + dtcol_ref[...])
    eg_all = jnp.exp(g_all)
    eg_ref[...] = eg_all
    lane = lax.broadcasted_iota(jnp.int32, (DEC_BATCH, 128), 1)
    qk_all = jnp.zeros((DEC_BATCH, 128), F32)
    for h in range(H):
        q = act[:, h * DN_DK:(h + 1) * DN_DK]
        k = act[:, (H + h) * DN_DK:(H + h + 1) * DN_DK]
        v = act[:, (2 * H + h) * DN_DK:(2 * H + h + 1) * DN_DK]
        q = q * lax.rsqrt(jnp.sum(q * q, axis=-1, keepdims=True) + EPS) * (DN_DK ** -0.5)
        k = k * lax.rsqrt(jnp.sum(k * k, axis=-1, keepdims=True) + EPS)
        beta = beta_all[:, h:h + 1]
        eg = eg_all[:, H + h:H + h + 1]
        vec_ref[:, h * 128:(h + 1) * 128] = k * (beta * eg)
        vec_ref[:, (H + h) * 128:(H + h + 1) * 128] = q * eg
        vec_ref[:, (2 * H + h) * 128:(2 * H + h + 1) * 128] = k
        vec_ref[:, (3 * H + h) * 128:(3 * H + h + 1) * 128] = v * beta
        qk = jnp.sum(q * k, axis=-1, keepdims=True)
        qk_all = jnp.where(lane == h, qk, qk_all)
    qk_out[...] = qk_all


def _sample_rows(mix, pool_t, conv_t, pool_w, pool_scale, gmlp_norm, ws_diag, gb0, conv_w,
                 acol, dtcol, ya, yb):
    rb = N_PROMPT // DEC_BATCH
    full = lambda shape: pl.BlockSpec(shape, lambda i: (0,) * len(shape))
    n = DEC_BATCH
    hbm = pl.BlockSpec(memory_space=pl.ANY)
    return pl.pallas_call(
        _sample_rows_kernel,
        grid=(1,),
        in_specs=[
            pl.BlockSpec((n, MIX_BLOCK), lambda i: (rb, 0)),
            pl.BlockSpec((n, MIX_BLOCK), lambda i: (rb, 1)),
            pl.BlockSpec((n, MIX_BLOCK), lambda i: (rb, 2)),
            pl.BlockSpec((n, MIX_BLOCK), lambda i: (rb, 3)),
            full((POOL_HIST, n, POOL_WIDTH)),
            full((CONV_W - 1, n, CONV_DIM)),
            full((4, GROUP, GROUP)),
            full((1, POOL_WIDTH)),
            full((1, GMLP_WIDTH)),
            full((1, GMLP_WIDTH)),
            full((1, GMLP_WIDTH)),
            full((CONV_W, CONV_DIM)),
            full((1, 128)),
            full((1, 128)),
            hbm,
            hbm,
        ],
        out_specs=[
            pl.BlockSpec((n, POOL_WIDTH), lambda i: (rb, 0)),
            pl.BlockSpec((n, GMLP_WIDTH), lambda i: (rb, 0)),
            full((n, GMLP_WIDTH)),
            full((POOL_HIST, n, POOL_WIDTH)),
            full((CONV_W - 1, n, CONV_DIM)),
            full((n, 4 * DN_HEADS * 128)),
            full((n, 128)),
            full((n, 128)),
        ],
        out_shape=[
            jax.ShapeDtypeStruct((N_ROWS, POOL_WIDTH), F32),
            jax.ShapeDtypeStruct((N_ROWS, GMLP_WIDTH), F32),
            jax.ShapeDtypeStruct((n, GMLP_WIDTH), F32),
            jax.ShapeDtypeStruct((POOL_HIST, n, POOL_WIDTH), F32),
            jax.ShapeDtypeStruct((CONV_W - 1, n, CONV_DIM), F32),
            jax.ShapeDtypeStruct((n, 4 * DN_HEADS * 128), F32),
            jax.ShapeDtypeStruct((n, 128), F32),
            jax.ShapeDtypeStruct((n, 128), F32),
        ],
        input_output_aliases={14: 0, 15: 1},
        compiler_params=pltpu.CompilerParams(
            dimension_semantics=("arbitrary",),
            vmem_limit_bytes=VMEM_LIMIT),
        name="sample_rows",
    )(mix, mix, mix, mix, pool_t, conv_t, pool_w, pool_scale, gmlp_norm,
      ws_diag, gb0, conv_w, acol, dtcol, ya, yb)


def _sample_state_kernel(vec_ref, scal_ref, z_ref, on_ref, s_ref, *rest):
    yc_ref, so_ref = rest[-2:]
    H = DN_HEADS
    row = lax.broadcasted_iota(jnp.int32, (1, 8, 1), 1)
    for h in range(H):
        s = s_ref[0, :, h]
        w = vec_ref[:, h:h + 1, :]
        qg = vec_ref[:, H + h:H + h + 1, :]
        k = vec_ref[:, 2 * H + h:2 * H + h + 1, :]
        vb = vec_ref[:, 3 * H + h:3 * H + h + 1, :]
        eg = scal_ref[:, h:h + 1, :]
        qk = scal_ref[:, H + h:H + h + 1, :]
        lhs = jnp.where(row == 0, w, jnp.where(row == 1, qg, 0.0))
        r = _bdot(lhs, s)
        v_new = vb - r[:, 0:1, :]
        o = r[:, 1:2, :] + qk * v_new
        k8 = jnp.where(row == 0, k, 0.0)
        v8 = jnp.broadcast_to(v_new, k8.shape)
        so_ref[0, :, h] = s * eg + _bdot_tn(k8, v8)
        o = o * lax.rsqrt(jnp.mean(o * o, axis=-1, keepdims=True) + EPS) * on_ref[...]
        yc_ref[:, h:h + 1, :] = o * _silu(z_ref[:, h:h + 1, :])


def _sample_state(layer, vec3, scal3, z3, onorm, state, stacked):
    sb = SAMPLE_BLOCK
    H = DN_HEADS
    in_specs = [
        pl.BlockSpec((sb, 4 * H, 128), lambda i: (i, 0, 0)),
        pl.BlockSpec((sb, 2 * H, 1), lambda i: (i, 0, 0)),
        pl.BlockSpec((sb, H, DN_DV), lambda i: (i, 0, 0)),
        pl.BlockSpec((1, DN_DV), lambda i: (0, 0)),
        pl.BlockSpec((1, sb, H, DN_DK, DN_DV), lambda i: (layer, i, 0, 0, 0)),
    ]
    args = [vec3, scal3, z3, onorm, state]
    aliases = {}
    if stacked is not None:
        in_specs.append(pl.BlockSpec(memory_space=pl.ANY))
        args.append(stacked)
        aliases = {5: 1}
    return pl.pallas_call(
        _sample_state_kernel,
        grid=(DEC_BATCH // sb,),
        in_specs=in_specs,
        out_specs=[
            pl.BlockSpec((sb, H, DN_DV), lambda i: (i, 0, 0)),
            pl.BlockSpec((1, sb, H, DN_DK, DN_DV), lambda i: (layer, i, 0, 0, 0)),
        ],
        out_shape=[
            jax.ShapeDtypeStruct((DEC_BATCH, H, DN_DV), F32),
            jax.ShapeDtypeStruct((DEPTH, DEC_BATCH, H, DN_DK, DN_DV), F32),
        ],
        input_output_aliases=aliases,
        compiler_params=pltpu.CompilerParams(
            dimension_semantics=("parallel",),
            vmem_limit_bytes=VMEM_LIMIT),
        name="sample_state",
    )(*args)


def _merge_kernel(x_ref, g0_ref, g1_ref, g2_ref, bg_ref, ya_ref, yb_ref, yc_ref,
                  pa_ref, pb_ref, pc_ref, wo_ref, *rest):
    o_ref = rest[-1]
    lane = lax.broadcasted_iota(jnp.int32, (1, GATE_WIN), 1)
    valid = (lane >= GATE_SHIFT) & (lane < GATE_SHIFT + D_MODEL)

    def branch(g_ref, i, y_ref, p_ref):
        gate = jax.nn.sigmoid(g_ref[...] + bg_ref[:, i * GATE_WIN:(i + 1) * GATE_WIN])
        gate = jnp.where(valid, gate, 0.0)
        return gate * jnp.dot(y_ref[...].astype(BF16), p_ref[...], preferred_element_type=F32)

    m = branch(g0_ref, 0, ya_ref, pa_ref)
    m = m + branch(g1_ref, 1, yb_ref, pb_ref)
    m = m + branch(g2_ref, 2, yc_ref, pc_ref)
    o_ref[...] = x_ref[...] + jnp.dot(m.astype(BF16), wo_ref[...], preferred_element_type=F32)


def _shift_cols_kernel(w_ref, o_ref):
    w = w_ref[...]
    wide = jnp.concatenate([w, jnp.zeros((w.shape[0], GATE_WIN - D_MODEL), F32)], axis=1)
    o_ref[...] = pltpu.roll(wide, GATE_SHIFT, axis=1).astype(BF16)


def _shift_cols(w):
    depth, k, _ = w.shape
    rows = 512
    return pl.pallas_call(
        _shift_cols_kernel,
        grid=(depth, k // rows),
        in_specs=[pl.BlockSpec((None, rows, D_MODEL), lambda l, r: (l, r, 0))],
        out_specs=pl.BlockSpec((None, rows, GATE_WIN), lambda l, r: (l, r, 0)),
        out_shape=jax.ShapeDtypeStruct((depth, k, GATE_WIN), BF16),
        compiler_params=pltpu.CompilerParams(dimension_semantics=("parallel", "parallel"),
                                             vmem_limit_bytes=VMEM_LIMIT),
        name="shift_cols",
    )(w)


def _shift_rows_kernel(w_ref, o_ref):
    tail = GATE_WIN - D_MODEL - GATE_SHIFT
    o_ref[0:GATE_SHIFT, :] = jnp.zeros((GATE_SHIFT, D_MODEL), BF16)
    o_ref[GATE_SHIFT:GATE_SHIFT + D_MODEL, :] = w_ref[...].astype(BF16)
    o_ref[GATE_SHIFT + D_MODEL:GATE_WIN, :] = jnp.zeros((tail, D_MODEL), BF16)


def _shift_rows(w):
    depth = w.shape[0]
    return pl.pallas_call(
        _shift_rows_kernel,
        grid=(depth,),
        in_specs=[pl.BlockSpec((None, D_MODEL, D_MODEL), lambda l: (l, 0, 0))],
        out_specs=pl.BlockSpec((None, GATE_WIN, D_MODEL), lambda l: (l, 0, 0)),
        out_shape=jax.ShapeDtypeStruct((depth, GATE_WIN, D_MODEL), BF16),
        compiler_params=pltpu.CompilerParams(dimension_semantics=("parallel",),
                                             vmem_limit_bytes=VMEM_LIMIT),
        name="shift_rows",
    )(w)


def _merge(layer, x, proj, b_gate, ya, yb, yc, pa, pb, pc, wo, tile, first, count, prev):
    once = lambda shape: pl.BlockSpec(shape, lambda i: (0,) * len(shape),
                                      pipeline_mode=pl.Buffered(1))
    slab = lambda k, n: pl.BlockSpec((None, k, n), lambda i: (layer, 0, 0),
                                     pipeline_mode=pl.Buffered(1))
    rows = lambda width: pl.BlockSpec((tile, width), lambda i: (first + i, 0))
    gate = lambda k: pl.BlockSpec((pl.Element(tile), pl.Element(GATE_WIN)),
                                  lambda i: ((first + i) * tile, COL_BA + k * D_MODEL))
    in_specs = [
        rows(D_MODEL),
        gate(0), gate(1), gate(2),
        once((1, 3 * GATE_WIN)),
        rows(POOL_WIDTH), rows(GMLP_WIDTH),
        pl.BlockSpec((tile, Z_WIDTH), lambda i: (i, 0)),
        slab(POOL_WIDTH, GATE_WIN),
        slab(GMLP_WIDTH, GATE_WIN),
        slab(Z_WIDTH, GATE_WIN),
        slab(GATE_WIN, D_MODEL),
    ]
    args = [x, proj, proj, proj, b_gate, ya, yb, yc, pa, pb, pc, wo]
    aliases = {}
    if prev is not None:
        in_specs.append(pl.BlockSpec(memory_space=pl.ANY))
        args.append(prev)
        aliases = {len(args) - 1: 0}
    return pl.pallas_call(
        _merge_kernel,
        grid=(count,),
        in_specs=in_specs,
        out_specs=rows(D_MODEL),
        out_shape=jax.ShapeDtypeStruct((N_ROWS, D_MODEL), F32),
        input_output_aliases=aliases,
        compiler_params=pltpu.CompilerParams(
            dimension_semantics=("parallel",),
            vmem_limit_bytes=VMEM_LIMIT),
        name="merge",
    )(*args)


def _final_kernel(x_ref, g_ref, o_ref):
    o_ref[...] = _rmsnorm(x_ref[...], g_ref[...])


def _final_norm(x, g, t, first, count):
    return pl.pallas_call(
        _final_kernel,
        grid=(count,),
        in_specs=[pl.BlockSpec((t, D_MODEL), lambda i: (first + i, 0)),
                  pl.BlockSpec((1, D_MODEL), lambda i: (0, 0))],
        out_specs=pl.BlockSpec((t, D_MODEL), lambda i: (i, 0)),
        out_shape=jax.ShapeDtypeStruct((count * t, D_MODEL), F32),
        compiler_params=pltpu.CompilerParams(dimension_semantics=("parallel",),
                                             vmem_limit_bytes=VMEM_LIMIT),
        name="final_norm",
    )(x, g)


def _lane_pad(v, offset, width=128):
    return jnp.zeros((1, width), F32).at[0, offset:offset + v.shape[0]].set(v)


def _tail_rows(mix, count, col, width):
    return jnp.stack([
        lax.slice(mix, ((b + 1) * SEQ - count, col), ((b + 1) * SEQ, col + width))
        for b in range(BATCH)])


def kernel(x_prompt, x_sample, state_delta, state_conv, state_pool, ffn1_norm, ffn1_wg, ffn1_wu, ffn1_wd, mix_norm, w_in, b_gate, pool_w, pool_scale, gmlp_norm, gmlp_ws, gmlp_b, dn_conv, dn_a_log, dn_dt_bias, dn_onorm, proj_a, proj_b, proj_c, w_o, ffn2_norm, ffn2_wg, ffn2_wu, ffn2_wd, final_norm):
    H = DN_HEADS
    x = jnp.concatenate([x_prompt.reshape(N_PROMPT, D_MODEL),
                         x_sample.reshape(DEC_BATCH, D_MODEL)], axis=0)
    causal = jnp.tril(jnp.ones((GMLP_CHUNK, GMLP_CHUNK), bool))
    outs_delta_p, outs_conv_p, outs_conv_s = [], [], []
    outs_pool_p, outs_pool_s, outs_zv = [], [], []
    delta_s = None
    w_in_t = jnp.swapaxes(w_in, 1, 2)
    weights = (_shift_cols(proj_a), _shift_cols(proj_b), _shift_cols(proj_c), _shift_rows(w_o))
    shift_cols = ((0, 0), (GATE_SHIFT, GATE_WIN - D_MODEL - GATE_SHIFT))
    for l in range(DEPTH):
        x, h_mix = _ffn(l, x, ffn1_norm[l][None], ffn1_wg, ffn1_wu, ffn1_wd,
                        next_gain=mix_norm[l][None])

        mix = _inproj(l, h_mix, w_in_t)

        pw = pool_w[l].astype(BF16)
        ps = pool_scale[l][None]
        gn = gmlp_norm[l][None]
        ws_tril = jnp.where(causal, gmlp_ws[l], 0).astype(BF16)
        gb_full = jnp.repeat(gmlp_b[l].T, GROUP, axis=1)
        ws_diag = jnp.repeat(gmlp_ws[l][:, 0, 0], GROUP)[None]
        gb0 = jnp.repeat(gmlp_b[l][:, 0], GROUP)[None]
        acol = _lane_pad(dn_a_log[l], H)
        dtcol = _lane_pad(dn_dt_bias[l], H)
        pair_rows = lambda v: jnp.pad(jnp.repeat(v.reshape(H // 2, 2), DN_CHUNK, axis=1),
                                      ((0, 8 - H // 2), (0, 0)))
        arow = pair_rows(dn_a_log[l])
        dtrow = pair_rows(dn_dt_bias[l])
        onorm = dn_onorm[l][None]

        ya, yb = _pool_gmlp(mix, pw, ps, gn, ws_tril, gb_full)
        a_p = lax.slice(mix, (0, COL_BA + H), (N_PROMPT, COL_BA + 2 * H))
        bat = a_p.reshape(N_PROMPT // DN_CHUNK, DN_CHUNK, H // 2, 2).transpose(0, 2, 3, 1)
        bat = jnp.pad(bat.reshape(N_PROMPT // DN_CHUNK, H // 2, 2 * DN_CHUNK),
                      ((0, 0), (0, 8 - H // 2), (0, 0)))
        yc_p, s_p = _delta(mix, bat, dn_conv[l], acol, dtcol, arow, dtrow, onorm)
        outs_delta_p.append(s_p)
        outs_conv_p.append(_tail_rows(mix, CONV_W - 1, COL_QKV, CONV_DIM))
        outs_pool_p.append(_tail_rows(mix, POOL_HIST, 0, POOL_WIDTH))

        pool_t = state_pool[l].transpose(1, 0, 2)
        conv_t = state_conv[l].transpose(1, 0, 2)
        ya, yb, zv_s, npool_t, nconv_t, vec, eg_all, qk_all = _sample_rows(
            mix, pool_t, conv_t, pw, ps, gn, ws_diag, gb0, dn_conv[l], acol, dtcol, ya, yb)
        vec3 = vec.reshape(DEC_BATCH, 4 * H, 128)
        scal3 = jnp.concatenate([eg_all[:, H:2 * H], qk_all[:, :H]], axis=1)[:, :, None]
        z3 = lax.slice(mix, (N_PROMPT, COL_Z), (N_ROWS, COL_Z + Z_WIDTH)).reshape(DEC_BATCH, H, DN_DV)
        yc_s3, delta_s = _sample_state(l, vec3, scal3, z3, onorm, state_delta, delta_s)
        outs_conv_s.append(nconv_t.transpose(1, 0, 2))
        outs_pool_s.append(npool_t.transpose(1, 0, 2))
        outs_zv.append(zv_s[:, None, :])

        bg = jnp.pad(b_gate[l].reshape(3, D_MODEL), shift_cols).reshape(1, 3 * GATE_WIN)
        x_new = _merge(l, x, mix, bg, ya, yb, yc_p, *weights,
                       tile=MERGE_TILE, first=0, count=N_PROMPT // MERGE_TILE, prev=None)
        x = _merge(l, x, mix, bg, ya, yb, yc_s3.reshape(DEC_BATCH, Z_WIDTH), *weights,
                   tile=DEC_BATCH, first=N_PROMPT // DEC_BATCH, count=1, prev=x_new)

        x = _ffn(l, x, ffn2_norm[l][None], ffn2_wg, ffn2_wu, ffn2_wd)

    g = final_norm[None]
    y_prompt = _final_norm(x, g, NORM_TILE, 0, N_PROMPT // NORM_TILE).reshape(BATCH, SEQ, D_MODEL)
    y_sample = _final_norm(x, g, DEC_BATCH, N_PROMPT // DEC_BATCH, 1).reshape(
        DEC_BATCH, 1, D_MODEL)
    return (y_prompt, y_sample, jnp.stack(outs_delta_p), delta_s,
            jnp.stack(outs_conv_p), jnp.stack(outs_conv_s), jnp.stack(outs_pool_p),
            jnp.stack(outs_pool_s), jnp.stack(outs_zv))
```

```python
import jax
import jax.numpy as jnp
from jax import lax
from jax.experimental import pallas as pl
from jax.experimental.pallas import tpu as pltpu

F32 = jnp.float32
BF16 = jnp.bfloat16

D_MODEL = 2048
BATCH = 4
SEQ = 2048
DEPTH = 4
DEC_BATCH = 128
EPS = 1e-6
FFN_DIM = 5632
POOL_WINDOWS = (2, 4, 8, 16)
POOL_WIDTH = 512
POOL_HIST = 15
GMLP_CHUNK = 128
GMLP_WIDTH = 512
DN_HEADS = 8
DN_DK = 128
DN_DV = 128
DN_CHUNK = 64
CONV_W = 4
CONV_DIM = 3072
GROUP = 128

N_PROMPT = BATCH * SEQ
N_ROWS = N_PROMPT + DEC_BATCH

COL_QKV = 1536
COL_Z = 4608
COL_BA = 5632
COL_GATE = 5648
IN_COLS = 11792
PROJ_COLS = 12288
MIX_BLOCK = 1536
Z_WIDTH = DN_HEADS * DN_DV
GATE_SHIFT = COL_GATE - COL_BA
GATE_WIN = D_MODEL + 128

ROW_TILE = 1040
MERGE_TILE = 256
FFN_TILE = 256
PROJ_ROWS = 2080
PROJ_TILE = 1024
NORM_TILE = 512
DELTA_ROWS = 512
POOL_ROWS = 512
SAMPLE_BLOCK = 8
VMEM_LIMIT = 56 * 1024 * 1024
FFN_VMEM_LIMIT = 60 * 1024 * 1024


def _rmsnorm(x, g):
    ms = jnp.mean(x * x, axis=-1, keepdims=True)
    return x * lax.rsqrt(ms + EPS) * g


def _silu(x):
    return x * jax.nn.sigmoid(x)


def _softplus(x):
    return jnp.maximum(x, 0.0) + jnp.log1p(jnp.exp(-jnp.abs(x)))


def _bdot(a, b):
    return jnp.einsum('hmk,hkn->hmn', a.astype(BF16), b.astype(BF16),
                      preferred_element_type=F32)


def _bdot_tn(a, b):
    return jnp.einsum('hcm,hcn->hmn', a.astype(BF16), b.astype(BF16),
                      preferred_element_type=F32)


def _split(x):
    hi = x.astype(BF16)
    lo = (x - hi.astype(F32)).astype(BF16)
    return hi, lo


def _ffn_kernel(x_ref, g_ref, wg_ref, wu_ref, wd_ref, *rest):
    emit_norm = len(rest) == 4
    if emit_norm:
        gn_ref, o_ref, n_ref, h_ref = rest
    else:
        o_ref, h_ref = rest
    j = pl.program_id(1)

    @pl.when(j == 0)
    def _():
        x = x_ref[...]
        h_ref[...] = _rmsnorm(x, g_ref[...]).astype(BF16)
        o_ref[...] = x

    h = h_ref[...]
    a = jnp.dot(h, wg_ref[...].astype(BF16), preferred_element_type=F32)
    b = jnp.dot(h, wu_ref[...].astype(BF16), preferred_element_type=F32)
    act = (_silu(a) * (0.5 * b)).astype(BF16)
    o_ref[...] += jnp.dot(act, wd_ref[...].astype(BF16), preferred_element_type=F32)

    if emit_norm:
        @pl.when(j == pl.num_programs(1) - 1)
        def _():
            n_ref[...] = _rmsnorm(o_ref[...], gn_ref[...]).astype(BF16)


def _ffn(layer, x, g, wg, wu, wd, next_gain=None, tm=ROW_TILE, tf=FFN_TILE, x_buffers=2):
    rows = x.shape[0]
    vec = pl.BlockSpec((1, D_MODEL), lambda i, j: (0, 0))
    tile = pl.BlockSpec((tm, D_MODEL), lambda i, j: (i, 0))
    in_specs = [
        pl.BlockSpec((tm, D_MODEL), lambda i, j: (i, 0), pipeline_mode=pl.Buffered(x_buffers)),
        vec,
        pl.BlockSpec((None, D_MODEL, tf), lambda i, j: (layer, 0, j)),
        pl.BlockSpec((None, D_MODEL, tf), lambda i, j: (layer, 0, j)),
        pl.BlockSpec((None, tf, D_MODEL), lambda i, j: (layer, j, 0)),
    ]
    args = [x, g, wg, wu, wd]
    out_specs = [tile]
    out_shape = [jax.ShapeDtypeStruct((rows, D_MODEL), F32)]
    if next_gain is not None:
        in_specs.append(vec)
        args.append(next_gain)
        out_specs.append(pl.BlockSpec((tm, D_MODEL), lambda i, j: (i, 0)))
        out_shape.append(jax.ShapeDtypeStruct((rows, D_MODEL), BF16))
    out = pl.pallas_call(
        _ffn_kernel,
        grid=(rows // tm, FFN_DIM // tf),
        in_specs=in_specs,
        out_specs=out_specs,
        out_shape=out_shape,
        scratch_shapes=[pltpu.VMEM((tm, D_MODEL), BF16)],
        compiler_params=pltpu.CompilerParams(
            dimension_semantics=("parallel", "arbitrary"),
            vmem_limit_bytes=FFN_VMEM_LIMIT),
        name="ffn",
    )(*args)
    return out if next_gain is not None else out[0]


def _inproj_kernel(h_ref, wt_ref, o_ref):
    col = pl.program_id(1) * PROJ_TILE + lax.broadcasted_iota(jnp.int32, (PROJ_TILE, 1), 0)
    wt = jnp.where(col < IN_COLS, wt_ref[...], 0.0).astype(BF16)
    o_ref[...] = lax.dot_general(h_ref[...], wt, (((1,), (1,)), ((), ())),
                                 preferred_element_type=F32)


def _inproj(layer, h, wt):
    rows = h.shape[0]
    tm = PROJ_ROWS
    return pl.pallas_call(
        _inproj_kernel,
        grid=(rows // tm, PROJ_COLS // PROJ_TILE),
        in_specs=[
            pl.BlockSpec((tm, D_MODEL), lambda i, j: (i, 0)),
            pl.BlockSpec((None, PROJ_TILE, D_MODEL), lambda i, j: (layer, j, 0)),
        ],
        out_specs=pl.BlockSpec((tm, PROJ_TILE), lambda i, j: (i, j)),
        out_shape=jax.ShapeDtypeStruct((rows, PROJ_COLS), F32),
        compiler_params=pltpu.CompilerParams(
            dimension_semantics=("parallel", "arbitrary"),
            vmem_limit_bytes=FFN_VMEM_LIMIT),
        name="inproj",
    )(h, wt)


def _gmlp_gate(gu, gv, norm_g):
    u = jax.nn.gelu(gu)
    zv = _rmsnorm(jax.nn.gelu(gv), norm_g)
    return u, zv


def _pool_gmlp_kernel(m_ref, pw_ref, ps_ref, gn_ref, ws_ref, gb_ref,
                      ya_ref, yb_ref, ext_ref):
    c = pl.program_id(1)
    hist = 16
    rows = POOL_ROWS

    @pl.when(c == 0)
    def _():
        ext_ref[0:hist, :] = jnp.zeros((hist, POOL_WIDTH), F32)

    a = m_ref[:, 0:POOL_WIDTH]
    ext_ref[hist:hist + rows, :] = a
    pos = c * rows + lax.broadcasted_iota(jnp.int32, (rows, 1), 0)
    for gi, w in enumerate(POOL_WINDOWS):
        sl = slice(gi * GROUP, (gi + 1) * GROUP)
        s = a[:, sl]
        for i in range(1, w):
            s = s + ext_ref[hist - i:hist - i + rows, sl]
        cnt = jnp.minimum(pos + 1, w).astype(F32)
        d = s / cnt - a[:, sl]
        y = jnp.dot(d.astype(BF16), pw_ref[gi], preferred_element_type=F32)
        ya_ref[:, sl] = y * ps_ref[:, sl]
    ext_ref[0:hist, :] = a[rows - hist:rows, :]

    u, zv = _gmlp_gate(m_ref[:, POOL_WIDTH:POOL_WIDTH + GMLP_WIDTH],
                       m_ref[:, POOL_WIDTH + GMLP_WIDTH:MIX_BLOCK], gn_ref[...])
    for ch in range(rows // GMLP_CHUNK):
        r = slice(ch * GMLP_CHUNK, (ch + 1) * GMLP_CHUNK)
        for gi in range(4):
            sl = slice(gi * GROUP, (gi + 1) * GROUP)
            s = jnp.dot(ws_ref[gi], zv[r, sl].astype(BF16), preferred_element_type=F32)
            yb_ref[r, sl] = u[r, sl] * (s + gb_ref[:, sl])


def _pool_gmlp(mix, pool_w, pool_scale, gmlp_norm, ws_tril, gb_full):
    chunks = SEQ // POOL_ROWS
    row = lambda b, c: b * chunks + c
    full = lambda shape: pl.BlockSpec(shape, lambda b, c: (0,) * len(shape))
    return pl.pallas_call(
        _pool_gmlp_kernel,
        grid=(BATCH, chunks),
        in_specs=[
            pl.BlockSpec((POOL_ROWS, MIX_BLOCK), lambda b, c: (row(b, c), 0)),
            full((4, GROUP, GROUP)),
            full((1, POOL_WIDTH)),
            full((1, GMLP_WIDTH)),
            full((4, GMLP_CHUNK, GMLP_CHUNK)),
            full((GMLP_CHUNK, GMLP_WIDTH)),
        ],
        out_specs=[
            pl.BlockSpec((POOL_ROWS, POOL_WIDTH), lambda b, c: (row(b, c), 0)),
            pl.BlockSpec((POOL_ROWS, GMLP_WIDTH), lambda b, c: (row(b, c), 0)),
        ],
        out_shape=[
            jax.ShapeDtypeStruct((N_ROWS, POOL_WIDTH), F32),
            jax.ShapeDtypeStruct((N_ROWS, GMLP_WIDTH), F32),
        ],
        scratch_shapes=[pltpu.VMEM((16 + POOL_ROWS, POOL_WIDTH), F32)],
        compiler_params=pltpu.CompilerParams(
            dimension_semantics=("parallel", "arbitrary"),
            vmem_limit_bytes=VMEM_LIMIT),
        name="pool_gmlp",
    )(mix, pool_w, pool_scale, gmlp_norm, ws_tril, gb_full)


def _pair_bd(y, left):
    zero = jnp.zeros_like(y)
    return jnp.concatenate([jnp.where(left, y, zero), jnp.where(left, zero, y)],
                           axis=1).astype(BF16)


def _pair_mm(x, y_bd):
    return _bdot(x, y_bd)


def _pair_inverse(m, left, blk, eye):
    d = jnp.where(blk, m, 0.0)
    low = m - d
    d2 = _pair_mm(d, _pair_bd(d, left))
    d2_bd = _pair_bd(d2, left)
    d4 = _pair_mm(d2, d2_bd)
    d4_bd = _pair_bd(d4, left)
    d8_bd = _pair_bd(_pair_mm(d4, d4_bd), left)
    td = eye - d
    td = td + _pair_mm(td, d2_bd)
    td = td + _pair_mm(td, d4_bd)
    td = td + _pair_mm(td, d8_bd)
    n = _pair_mm(td, _pair_bd(low, left))
    n2 = _pair_mm(n, _pair_bd(n, left))
    x = eye - n + n2 - _pair_mm(n, _pair_bd(n2, left))
    t0 = _pair_mm(x, _pair_bd(td, left))
    th, tl = _split(t0)
    mh, ml = _split(m)
    th_bd = _pair_bd(th, left)
    tl_bd = _pair_bd(tl, left)
    mt = _bdot(mh, th_bd) + (_bdot(mh, tl_bd) + _bdot(ml, th_bd))
    r = eye - t0 - mt
    return t0 + _bdot(th, _pair_bd(r, left))


def _cumsum_exact(g, tri, dims):
    g1 = g.astype(BF16)
    r1 = g - g1.astype(F32)
    g2 = r1.astype(BF16)
    g3 = (r1 - g2.astype(F32)).astype(BF16)
    if dims == 'rows':
        dot = lambda t: jnp.dot(tri, t, preferred_element_type=F32)
    else:
        dot = lambda t: jnp.dot(t, tri, preferred_element_type=F32)
    return dot(g1) + (dot(g2) + dot(g3))


def _delta_kernel(qk_ref, kv_ref, zb_ref, bat_ref, cw_ref, acol_ref, dtcol_ref,
                  arow_ref, dtrow_ref, on_ref, yc_ref, s_ref, ext_ref):
    n = pl.program_id(1)
    C = DN_CHUNK
    H = DN_HEADS
    R = DELTA_ROWS

    @pl.when(n == 0)
    def _():
        ext_ref[0:8, :] = jnp.zeros((8, CONV_DIM), F32)
        s_ref[...] = jnp.zeros(s_ref.shape, F32)

    ext_ref[8:8 + R, 0:MIX_BLOCK] = qk_ref[...]
    ext_ref[8:8 + R, MIX_BLOCK:CONV_DIM] = kv_ref[...]
    conv = cw_ref[3:4, :] * ext_ref[8:8 + R, :]
    for i in range(CONV_W - 1):
        conv = conv + cw_ref[i:i + 1, :] * ext_ref[5 + i:5 + i + R, :]
    ext_ref[0:8, :] = ext_ref[R:R + 8, :]
    act = _silu(conv)

    ri = lax.broadcasted_iota(jnp.int32, (C, 2 * C), 0)
    li = lax.broadcasted_iota(jnp.int32, (C, 2 * C), 1)
    ci = li % C
    left = li < C
    causal = ri >= ci
    strict = ri > ci
    blk = (ri // 16) == (ci // 16)
    eye = jnp.where(ri == ci, 1.0, 0.0).astype(F32)
    zero = jnp.zeros((C, DN_DK), F32)

    rr = lax.broadcasted_iota(jnp.int32, (R, R), 0)
    cc = lax.broadcasted_iota(jnp.int32, (R, R), 1)
    tril_chunks = jnp.where((rr // C == cc // C) & (rr >= cc), 1.0, 0.0).astype(BF16)
    pr = lax.broadcasted_iota(jnp.int32, (2 * C, 2 * C), 0)
    pc = lax.broadcasted_iota(jnp.int32, (2 * C, 2 * C), 1)
    triu_pair = jnp.where((pr // C == pc // C) & (pr <= pc), 1.0, 0.0).astype(BF16)

    ba = zb_ref[:, Z_WIDTH:Z_WIDTH + 128]
    beta_all = jax.nn.sigmoid(ba)
    g_all = -jnp.exp(acol_ref[...]) * _softplus(ba + dtcol_ref[...])
    gc_all = _cumsum_exact(g_all, tril_chunks, 'rows')
    on = on_ref[...]

    NC = R // C
    NP = H // 2
    lane2 = lambda a, b: jnp.concatenate([a, b], axis=1)
    diag2 = lambda a, b: jnp.concatenate([lane2(a, jnp.zeros_like(b)),
                                          lane2(jnp.zeros_like(a), b)], axis=0)

    ks, kbs, qes, gcs = [], [], [], []
    kq_lhs, kq_rhs, gc_pair, gr_pair, uw_rhs = [], [], [], [], []
    for c in range(NC):
        rows = slice(c * C, (c + 1) * C)
        heads = lambda base: jnp.stack(
            [act[rows, (base + h) * DN_DK:(base + h + 1) * DN_DK] for h in range(H)])
        q = heads(0)
        k = heads(H)
        v = heads(2 * H)
        q = q * lax.rsqrt(jnp.sum(q * q, axis=-1, keepdims=True) + EPS) * (DN_DK ** -0.5)
        k = k * lax.rsqrt(jnp.sum(k * k, axis=-1, keepdims=True) + EPS)
        beta = jnp.stack([beta_all[rows, h:h + 1] for h in range(H)])
        gc = jnp.stack([gc_all[rows, H + h:H + h + 1] for h in range(H)])
        egc = jnp.exp(gc)
        kb = k * beta
        vb = v * beta
        kbe = kb * egc
        ks.append(k)
        kbs.append(kb)
        qes.append(q * egc)
        gcs.append(gc)
        g_t = -jnp.exp(arow_ref[...]) * _softplus(bat_ref[c] + dtrow_ref[...])
        gr_all = _cumsum_exact(g_t, triu_pair, 'cols')
        for p in range(NP):
            a, b = 2 * p, 2 * p + 1
            kq_lhs.append(jnp.concatenate([lane2(kb[a], kb[b]), lane2(q[a], q[b])], axis=0))
            kq_rhs.append(diag2(k[a], k[b]))
            gc_pair.append(jnp.where(left, gc[a], gc[b]))
            gr_pair.append(gr_all[p:p + 1, :])
            uw_rhs.append(diag2(lane2(vb[a], kbe[a]), lane2(vb[b], kbe[b])))

    kq = jnp.einsum('bmk,bnk->bmn', jnp.stack(kq_lhs).astype(BF16), jnp.stack(kq_rhs).astype(BF16),
                    preferred_element_type=F32)
    diff = jnp.stack(gc_pair) - jnp.stack(gr_pair)
    decay = jnp.where(causal, jnp.exp(jnp.where(causal, diff, 0.0)), 0.0)
    m = jnp.where(strict, kq[:, 0:C] * decay, 0.0)
    a_mat = kq[:, C:2 * C] * decay
    t = _pair_inverse(m, left, blk, eye)
    uw = _bdot(t, jnp.stack(uw_rhs))

    for c in range(NC):
        rows = slice(c * C, (c + 1) * C)
        part = lambda h, i: uw[c * NP + h // 2][:, (2 * (h % 2) + i) * DN_DV:
                                               (2 * (h % 2) + i + 1) * DN_DV]
        u = jnp.stack([part(h, 0) for h in range(H)])
        w = jnp.stack([part(h, 1) for h in range(H)])
        s = s_ref[0]
        ws_qs = _bdot(jnp.concatenate([w, qes[c]], axis=1), s)
        v_new = u - ws_qs[:, 0:C]
        glast = gcs[c][:, C - 1:C, :]
        kg = ks[c] * jnp.exp(glast - gcs[c])
        s_ref[0] = s * jnp.exp(glast) + _bdot_tn(kg, v_new)
        vn_bd = jnp.stack([diag2(v_new[2 * p], v_new[2 * p + 1]) for p in range(NP)])
        av = _bdot(a_mat[c * NP:(c + 1) * NP], vn_bd)
        o = ws_qs[:, C:2 * C] + jnp.stack(
            [av[h // 2][:, (h % 2) * DN_DV:(h % 2 + 1) * DN_DV] for h in range(H)])
        o = o * lax.rsqrt(jnp.mean(o * o, axis=-1, keepdims=True) + EPS) * on
        for h in range(H):
            sl = slice(h * DN_DV, (h + 1) * DN_DV)
            yc_ref[rows, sl] = o[h] * _silu(zb_ref[rows, sl])


def _delta(mix, bat, conv_w, acol, dtcol, arow, dtrow, onorm):
    steps = SEQ // DELTA_ROWS
    cps = DELTA_ROWS // DN_CHUNK
    row = lambda b, n: b * steps + n
    full = lambda shape: pl.BlockSpec(shape, lambda b, n: (0,) * len(shape))
    return pl.pallas_call(
        _delta_kernel,
        grid=(BATCH, steps),
        in_specs=[
            pl.BlockSpec((DELTA_ROWS, MIX_BLOCK), lambda b, n: (row(b, n), 1)),
            pl.BlockSpec((DELTA_ROWS, MIX_BLOCK), lambda b, n: (row(b, n), 2)),
            pl.BlockSpec((DELTA_ROWS, MIX_BLOCK), lambda b, n: (row(b, n), 3)),
            pl.BlockSpec((cps, 8, 2 * DN_CHUNK), lambda b, n: (row(b, n), 0, 0)),
            full((CONV_W, CONV_DIM)),
            full((1, 128)),
            full((1, 128)),
            full((8, 2 * DN_CHUNK)),
            full((8, 2 * DN_CHUNK)),
            full((1, DN_DV)),
        ],
        out_specs=[
            pl.BlockSpec((DELTA_ROWS, Z_WIDTH), lambda b, n: (row(b, n), 0)),
            pl.BlockSpec((1, DN_HEADS, DN_DK, DN_DV), lambda b, n: (b, 0, 0, 0)),
        ],
        out_shape=[
            jax.ShapeDtypeStruct((N_PROMPT, Z_WIDTH), F32),
            jax.ShapeDtypeStruct((BATCH, DN_HEADS, DN_DK, DN_DV), F32),
        ],
        scratch_shapes=[pltpu.VMEM((8 + DELTA_ROWS, CONV_DIM), F32)],
        compiler_params=pltpu.CompilerParams(
            dimension_semantics=("parallel", "arbitrary"),
            vmem_limit_bytes=VMEM_LIMIT),
        name="delta",
    )(mix, mix, mix, bat, conv_w, acol, dtcol, arow, dtrow, onorm)


def _sample_rows_kernel(m_ref, qk_ref, kv_ref, zb_ref, pool_ref, cst_ref,
                        pw_ref, ps_ref, gn_ref, wsd_ref, gb0_ref, cw_ref, acol_ref, dtcol_ref,
                        ya_in, yb_in,
                        ya_ref, yb_ref, zv_ref, npool_ref, nconv_ref, vec_ref, eg_ref, qk_out):
    del ya_in, yb_in
    H = DN_HEADS
    a = m_ref[:, 0:POOL_WIDTH]
    for gi, w in enumerate(POOL_WINDOWS):
        sl = slice(gi * GROUP, (gi + 1) * GROUP)
        s = a[:, sl]
        for i in range(1, w):
            s = s + pool_ref[POOL_HIST - i, :, sl]
        d = s / float(w) - a[:, sl]
        y = jnp.dot(d.astype(BF16), pw_ref[gi], preferred_element_type=F32)
        ya_ref[:, sl] = y * ps_ref[:, sl]
    for i in range(POOL_HIST - 1):
        npool_ref[i] = pool_ref[i + 1]
    npool_ref[POOL_HIST - 1] = a

    u, zv = _gmlp_gate(m_ref[:, POOL_WIDTH:POOL_WIDTH + GMLP_WIDTH],
                       m_ref[:, POOL_WIDTH + GMLP_WIDTH:MIX_BLOCK], gn_ref[...])
    zv_ref[...] = zv
    yb_ref[...] = u * (wsd_ref[...] * zv + gb0_ref[...])

    x = jnp.concatenate([qk_ref[...], kv_ref[...]], axis=-1)
    conv = cw_ref[3:4, :] * x
    for i in range(CONV_W - 1):
        conv = conv + cw_ref[i:i + 1, :] * cst_ref[i]
    nconv_ref[0] = cst_ref[1]
    nconv_ref[1] = cst_ref[2]
    nconv_ref[2] = x
    act = _silu(conv)
    ba = zb_ref[:, Z_WIDTH:Z_WIDTH + 128]
    beta_all = jax.nn.sigmoid(ba)
    g_all = -jnp.exp(acol_ref[...]) * _softplus(ba + dtcol_ref[...])
    eg_all = jnp.exp(g_all)
    eg_ref[...] = eg_all
    lane = lax.broadcasted_iota(jnp.int32, (DEC_BATCH, 128), 1)
    qk_all = jnp.zeros((DEC_BATCH, 128), F32)
    for h in range(H):
        q = act[:, h * DN_DK:(h + 1) * DN_DK]
        k = act[:, (H + h) * DN_DK:(H + h + 1) * DN_DK]
        v = act[:, (2 * H + h) * DN_DK:(2 * H + h + 1) * DN_DK]
        q = q * lax.rsqrt(jnp.sum(q * q, axis=-1, keepdims=True) + EPS) * (DN_DK ** -0.5)
        k = k * lax.rsqrt(jnp.sum(k * k, axis=-1, keepdims=True) + EPS)
        beta = beta_all[:, h:h + 1]
        eg = eg_all[:, H + h:H + h + 1]
        vec_ref[:, h * 128:(h + 1) * 128] = k * (beta * eg)
        vec_ref[:, (H + h) * 128:(H + h + 1) * 128] = q * eg
        vec_ref[:, (2 * H + h) * 128:(2 * H + h + 1) * 128] = k
        vec_ref[:, (3 * H + h) * 128:(3 * H + h + 1) * 128] = v * beta
        qk = jnp.sum(q * k, axis=-1, keepdims=True)
        qk_all = jnp.where(lane == h, qk, qk_all)
    qk_out[...] = qk_all


def _sample_rows(mix, pool_t, conv_t, pool_w, pool_scale, gmlp_norm, ws_diag, gb0, conv_w,
                 acol, dtcol, ya, yb):
    rb = N_PROMPT // DEC_BATCH
    full = lambda shape: pl.BlockSpec(shape, lambda i: (0,) * len(shape))
    n = DEC_BATCH
    hbm = pl.BlockSpec(memory_space=pl.ANY)
    return pl.pallas_call(
        _sample_rows_kernel,
        grid=(1,),
        in_specs=[
            pl.BlockSpec((n, MIX_BLOCK), lambda i: (rb, 0)),
            pl.BlockSpec((n, MIX_BLOCK), lambda i: (rb, 1)),
            pl.BlockSpec((n, MIX_BLOCK), lambda i: (rb, 2)),
            pl.BlockSpec((n, MIX_BLOCK), lambda i: (rb, 3)),
            full((POOL_HIST, n, POOL_WIDTH)),
            full((CONV_W - 1, n, CONV_DIM)),
            full((4, GROUP, GROUP)),
            full((1, POOL_WIDTH)),
            full((1, GMLP_WIDTH)),
            full((1, GMLP_WIDTH)),
            full((1, GMLP_WIDTH)),
            full((CONV_W, CONV_DIM)),
            full((1, 128)),
            full((1, 128)),
            hbm,
            hbm,
        ],
        out_specs=[
            pl.BlockSpec((n, POOL_WIDTH), lambda i: (rb, 0)),
            pl.BlockSpec((n, GMLP_WIDTH), lambda i: (rb, 0)),
            full((n, GMLP_WIDTH)),
            full((POOL_HIST, n, POOL_WIDTH)),
            full((CONV_W - 1, n, CONV_DIM)),
            full((n, 4 * DN_HEADS * 128)),
            full((n, 128)),
            full((n, 128)),
        ],
        out_shape=[
            jax.ShapeDtypeStruct((N_ROWS, POOL_WIDTH), F32),
            jax.ShapeDtypeStruct((N_ROWS, GMLP_WIDTH), F32),
            jax.ShapeDtypeStruct((n, GMLP_WIDTH), F32),
            jax.ShapeDtypeStruct((POOL_HIST, n, POOL_WIDTH), F32),
            jax.ShapeDtypeStruct((CONV_W - 1, n, CONV_DIM), F32),
            jax.ShapeDtypeStruct((n, 4 * DN_HEADS * 128), F32),
            jax.ShapeDtypeStruct((n, 128), F32),
            jax.ShapeDtypeStruct((n, 128), F32),
        ],
        input_output_aliases={14: 0, 15: 1},
        compiler_params=pltpu.CompilerParams(
            dimension_semantics=("arbitrary",),
            vmem_limit_bytes=VMEM_LIMIT),
        name="sample_rows",
    )(mix, mix, mix, mix, pool_t, conv_t, pool_w, pool_scale, gmlp_norm,
      ws_diag, gb0, conv_w, acol, dtcol, ya, yb)


def _sample_state_kernel(vec_ref, scal_ref, z_ref, on_ref, s_ref, *rest):
    yc_ref, so_ref = rest[-2:]
    H = DN_HEADS
    row = lax.broadcasted_iota(jnp.int32, (1, 8, 1), 1)
    for h in range(H):
        s = s_ref[0, :, h]
        w = vec_ref[:, h:h + 1, :]
        qg = vec_ref[:, H + h:H + h + 1, :]
        k = vec_ref[:, 2 * H + h:2 * H + h + 1, :]
        vb = vec_ref[:, 3 * H + h:3 * H + h + 1, :]
        eg = scal_ref[:, h:h + 1, :]
        qk = scal_ref[:, H + h:H + h + 1, :]
        lhs = jnp.where(row == 0, w, jnp.where(row == 1, qg, 0.0))
        r = _bdot(lhs, s)
        v_new = vb - r[:, 0:1, :]
        o = r[:, 1:2, :] + qk * v_new
        k8 = jnp.where(row == 0, k, 0.0)
        v8 = jnp.broadcast_to(v_new, k8.shape)
        so_ref[0, :, h] = s * eg + _bdot_tn(k8, v8)
        o = o * lax.rsqrt(jnp.mean(o * o, axis=-1, keepdims=True) + EPS) * on_ref[...]
        yc_ref[:, h:h + 1, :] = o * _silu(z_ref[:, h:h + 1, :])


def _sample_state(layer, vec3, scal3, z3, onorm, state, stacked):
    sb = SAMPLE_BLOCK
    H = DN_HEADS
    in_specs = [
        pl.BlockSpec((sb, 4 * H, 128), lambda i: (i, 0, 0)),
        pl.BlockSpec((sb, 2 * H, 1), lambda i: (i, 0, 0)),
        pl.BlockSpec((sb, H, DN_DV), lambda i: (i, 0, 0)),
        pl.BlockSpec((1, DN_DV), lambda i: (0, 0)),
        pl.BlockSpec((1, sb, H, DN_DK, DN_DV), lambda i: (layer, i, 0, 0, 0)),
    ]
    args = [vec3, scal3, z3, onorm, state]
    aliases = {}
    if stacked is not None:
        in_specs.append(pl.BlockSpec(memory_space=pl.ANY))
        args.append(stacked)
        aliases = {5: 1}
    return pl.pallas_call(
        _sample_state_kernel,
        grid=(DEC_BATCH // sb,),
        in_specs=in_specs,
        out_specs=[
            pl.BlockSpec((sb, H, DN_DV), lambda i: (i, 0, 0)),
            pl.BlockSpec((1, sb, H, DN_DK, DN_DV), lambda i: (layer, i, 0, 0, 0)),
        ],
        out_shape=[
            jax.ShapeDtypeStruct((DEC_BATCH, H, DN_DV), F32),
            jax.ShapeDtypeStruct((DEPTH, DEC_BATCH, H, DN_DK, DN_DV), F32),
        ],
        input_output_aliases=aliases,
        compiler_params=pltpu.CompilerParams(
            dimension_semantics=("parallel",),
            vmem_limit_bytes=VMEM_LIMIT),
        name="sample_state",
    )(*args)


def _merge_kernel(x_ref, g0_ref, g1_ref, g2_ref, bg_ref, ya_ref, yb_ref, yc_ref,
                  pa_ref, pb_ref, pc_ref, wo_ref, *rest):
    o_ref = rest[-1]
    lane = lax.broadcasted_iota(jnp.int32, (1, GATE_WIN), 1)
    valid = (lane >= GATE_SHIFT) & (lane < GATE_SHIFT + D_MODEL)

    def branch(g_ref, i, y_ref, p_ref):
        gate = jax.nn.sigmoid(g_ref[...] + bg_ref[:, i * GATE_WIN:(i + 1) * GATE_WIN])
        gate = jnp.where(valid, gate, 0.0)
        return gate * jnp.dot(y_ref[...].astype(BF16), p_ref[...], preferred_element_type=F32)

    m = branch(g0_ref, 0, ya_ref, pa_ref)
    m = m + branch(g1_ref, 1, yb_ref, pb_ref)
    m = m + branch(g2_ref, 2, yc_ref, pc_ref)
    o_ref[...] = x_ref[...] + jnp.dot(m.astype(BF16), wo_ref[...], preferred_element_type=F32)


def _shift_cols_kernel(w_ref, o_ref):
    w = w_ref[...]
    wide = jnp.concatenate([w, jnp.zeros((w.shape[0], GATE_WIN - D_MODEL), F32)], axis=1)
    o_ref[...] = pltpu.roll(wide, GATE_SHIFT, axis=1).astype(BF16)


def _shift_cols(w):
    depth, k, _ = w.shape
    rows = 512
    return pl.pallas_call(
        _shift_cols_kernel,
        grid=(depth, k // rows),
        in_specs=[pl.BlockSpec((None, rows, D_MODEL), lambda l, r: (l, r, 0))],
        out_specs=pl.BlockSpec((None, rows, GATE_WIN), lambda l, r: (l, r, 0)),
        out_shape=jax.ShapeDtypeStruct((depth, k, GATE_WIN), BF16),
        compiler_params=pltpu.CompilerParams(dimension_semantics=("parallel", "parallel"),
                                             vmem_limit_bytes=VMEM_LIMIT),
        name="shift_cols",
    )(w)


def _shift_rows_kernel(w_ref, o_ref):
    tail = GATE_WIN - D_MODEL - GATE_SHIFT
    o_ref[0:GATE_SHIFT, :] = jnp.zeros((GATE_SHIFT, D_MODEL), BF16)
    o_ref[GATE_SHIFT:GATE_SHIFT + D_MODEL, :] = w_ref[...].astype(BF16)
    o_ref[GATE_SHIFT + D_MODEL:GATE_WIN, :] = jnp.zeros((tail, D_MODEL), BF16)


def _shift_rows(w):
    depth = w.shape[0]
    return pl.pallas_call(
        _shift_rows_kernel,
        grid=(depth,),
        in_specs=[pl.BlockSpec((None, D_MODEL, D_MODEL), lambda l: (l, 0, 0))],
        out_specs=pl.BlockSpec((None, GATE_WIN, D_MODEL), lambda l: (l, 0, 0)),
        out_shape=jax.ShapeDtypeStruct((depth, GATE_WIN, D_MODEL), BF16),
        compiler_params=pltpu.CompilerParams(dimension_semantics=("parallel",),
                                             vmem_limit_bytes=VMEM_LIMIT),
        name="shift_rows",
    )(w)


def _merge(layer, x, proj, b_gate, ya, yb, yc, pa, pb, pc, wo, tile, first, count, prev):
    once = lambda shape: pl.BlockSpec(shape, lambda i: (0,) * len(shape),
                                      pipeline_mode=pl.Buffered(1))
    slab = lambda k, n: pl.BlockSpec((None, k, n), lambda i: (layer, 0, 0),
                                     pipeline_mode=pl.Buffered(1))
    rows = lambda width: pl.BlockSpec((tile, width), lambda i: (first + i, 0))
    gate = lambda k: pl.BlockSpec((pl.Element(tile), pl.Element(GATE_WIN)),
                                  lambda i: ((first + i) * tile, COL_BA + k * D_MODEL))
    in_specs = [
        rows(D_MODEL),
        gate(0), gate(1), gate(2),
        once((1, 3 * GATE_WIN)),
        rows(POOL_WIDTH), rows(GMLP_WIDTH),
        pl.BlockSpec((tile, Z_WIDTH), lambda i: (i, 0)),
        slab(POOL_WIDTH, GATE_WIN),
        slab(GMLP_WIDTH, GATE_WIN),
        slab(Z_WIDTH, GATE_WIN),
        slab(GATE_WIN, D_MODEL),
    ]
    args = [x, proj, proj, proj, b_gate, ya, yb, yc, pa, pb, pc, wo]
    aliases = {}
    if prev is not None:
        in_specs.append(pl.BlockSpec(memory_space=pl.ANY))
        args.append(prev)
        aliases = {len(args) - 1: 0}
    return pl.pallas_call(
        _merge_kernel,
        grid=(count,),
        in_specs=in_specs,
        out_specs=rows(D_MODEL),
        out_shape=jax.ShapeDtypeStruct((N_ROWS, D_MODEL), F32),
        input_output_aliases=aliases,
        compiler_params=pltpu.CompilerParams(
            dimension_semantics=("parallel",),
            vmem_limit_bytes=VMEM_LIMIT),
        name="merge",
    )(*args)


def _final_kernel(x_ref, g_ref, o_ref):
    o_ref[...] = _rmsnorm(x_ref[...], g_ref[...])


def _final_norm(x, g, t, first, count):
    return pl.pallas_call(
        _final_kernel,
        grid=(count,),
        in_specs=[pl.BlockSpec((t, D_MODEL), lambda i: (first + i, 0)),
                  pl.BlockSpec((1, D_MODEL), lambda i: (0, 0))],
        out_specs=pl.BlockSpec((t, D_MODEL), lambda i: (i, 0)),
        out_shape=jax.ShapeDtypeStruct((count * t, D_MODEL), F32),
        compiler_params=pltpu.CompilerParams(dimension_semantics=("parallel",),
                                             vmem_limit_bytes=VMEM_LIMIT),
        name="final_norm",
    )(x, g)


def _lane_pad(v, offset, width=128):
    return jnp.zeros((1, width), F32).at[0, offset:offset + v.shape[0]].set(v)


def _tail_rows(mix, count, col, width):
    return jnp.stack([
        lax.slice(mix, ((b + 1) * SEQ - count, col), ((b + 1) * SEQ, col + width))
        for b in range(BATCH)])


def kernel(x_prompt, x_sample, state_delta, state_conv, state_pool, ffn1_norm, ffn1_wg, ffn1_wu, ffn1_wd, mix_norm, w_in, b_gate, pool_w, pool_scale, gmlp_norm, gmlp_ws, gmlp_b, dn_conv, dn_a_log, dn_dt_bias, dn_onorm, proj_a, proj_b, proj_c, w_o, ffn2_norm, ffn2_wg, ffn2_wu, ffn2_wd, final_norm):
    H = DN_HEADS
    x = jnp.concatenate([x_prompt.reshape(N_PROMPT, D_MODEL),
                         x_sample.reshape(DEC_BATCH, D_MODEL)], axis=0)
    causal = jnp.tril(jnp.ones((GMLP_CHUNK, GMLP_CHUNK), bool))
    outs_delta_p, outs_conv_p, outs_conv_s = [], [], []
    outs_pool_p, outs_pool_s, outs_zv = [], [], []
    delta_s = None
    w_in_t = jnp.swapaxes(w_in, 1, 2)
    weights = (_shift_cols(proj_a), _shift_cols(proj_b), _shift_cols(proj_c), _shift_rows(w_o))
    shift_cols = ((0, 0), (GATE_SHIFT, GATE_WIN - D_MODEL - GATE_SHIFT))
    for l in range(DEPTH):
        x, h_mix = _ffn(l, x, ffn1_norm[l][None], ffn1_wg, ffn1_wu, ffn1_wd,
                        next_gain=mix_norm[l][None])

        mix = _inproj(l, h_mix, w_in_t)

        pw = pool_w[l].astype(BF16)
        ps = pool_scale[l][None]
        gn = gmlp_norm[l][None]
        ws_tril = jnp.where(causal, gmlp_ws[l], 0).astype(BF16)
        gb_full = jnp.repeat(gmlp_b[l].T, GROUP, axis=1)
        ws_diag = jnp.repeat(gmlp_ws[l][:, 0, 0], GROUP)[None]
        gb0 = jnp.repeat(gmlp_b[l][:, 0], GROUP)[None]
        acol = _lane_pad(dn_a_log[l], H)
        dtcol = _lane_pad(dn_dt_bias[l], H)
        pair_rows = lambda v: jnp.pad(jnp.repeat(v.reshape(H // 2, 2), DN_CHUNK, axis=1),
                                      ((0, 8 - H // 2), (0, 0)))
        arow = pair_rows(dn_a_log[l])
        dtrow = pair_rows(dn_dt_bias[l])
        onorm = dn_onorm[l][None]

        ya, yb = _pool_gmlp(mix, pw, ps, gn, ws_tril, gb_full)
        a_p = lax.slice(mix, (0, COL_BA + H), (N_PROMPT, COL_BA + 2 * H))
        bat = a_p.reshape(N_PROMPT // DN_CHUNK, DN_CHUNK, H // 2, 2).transpose(0, 2, 3, 1)
        bat = jnp.pad(bat.reshape(N_PROMPT // DN_CHUNK, H // 2, 2 * DN_CHUNK),
                      ((0, 0), (0, 8 - H // 2), (0, 0)))
        yc_p, s_p = _delta(mix, bat, dn_conv[l], acol, dtcol, arow, dtrow, onorm)
        outs_delta_p.append(s_p)
        outs_conv_p.append(_tail_rows(mix, CONV_W - 1, COL_QKV, CONV_DIM))
        outs_pool_p.append(_tail_rows(mix, POOL_HIST, 0, POOL_WIDTH))

        pool_t = state_pool[l].transpose(1, 0, 2)
        conv_t = state_conv[l].transpose(1, 0, 2)
        ya, yb, zv_s, npool_t, nconv_t, vec, eg_all, qk_all = _sample_rows(
            mix, pool_t, conv_t, pw, ps, gn, ws_diag, gb0, dn_conv[l], acol, dtcol, ya, yb)
        vec3 = vec.reshape(DEC_BATCH, 4 * H, 128)
        scal3 = jnp.concatenate([eg_all[:, H:2 * H], qk_all[:, :H]], axis=1)[:, :, None]
        z3 = lax.slice(mix, (N_PROMPT, COL_Z), (N_ROWS, COL_Z + Z_WIDTH)).reshape(DEC_BATCH, H, DN_DV)
        yc_s3, delta_s = _sample_state(l, vec3, scal3, z3, onorm, state_delta, delta_s)
        outs_conv_s.append(nconv_t.transpose(1, 0, 2))
        outs_pool_s.append(npool_t.transpose(1, 0, 2))
        outs_zv.append(zv_s[:, None, :])

        bg = jnp.pad(b_gate[l].reshape(3, D_MODEL), shift_cols).reshape(1, 3 * GATE_WIN)
        x_new = _merge(l, x, mix, bg, ya, yb, yc_p, *weights,
                       tile=MERGE_TILE, first=0, count=N_PROMPT // MERGE_TILE, prev=None)
        x = _merge(l, x, mix, bg, ya, yb, yc_s3.reshape(DEC_BATCH, Z_WIDTH), *weights,
                   tile=DEC_BATCH, first=N_PROMPT // DEC_BATCH, count=1, prev=x_new)

        x = _ffn(l, x, ffn2_norm[l][None], ffn2_wg, ffn2_wu, ffn2_wd)

    g = final_norm[None]
    y_prompt = _final_norm(x, g, NORM_TILE, 0, N_PROMPT // NORM_TILE).reshape(BATCH, SEQ, D_MODEL)
    y_sample = _final_norm(x, g, DEC_BATCH, N_PROMPT // DEC_BATCH, 1).reshape(
        DEC_BATCH, 1, D_MODEL)
    return (y_prompt, y_sample, jnp.stack(outs_delta_p), delta_s,
            jnp.stack(outs_conv_p), jnp.stack(outs_conv_s), jnp.stack(outs_pool_p),
            jnp.stack(outs_pool_s), jnp.stack(outs_zv))
```

```python
import jax
import jax.numpy as jnp
from jax import lax
from jax.experimental import pallas as pl
from jax.experimental.pallas import tpu as pltpu

F32 = jnp.float32
BF16 = jnp.bfloat16

D_MODEL = 2048
BATCH = 4
SEQ = 2048
DEPTH = 4
DEC_BATCH = 128
EPS = 1e-6
FFN_DIM = 5632
POOL_WINDOWS = (2, 4, 8, 16)
POOL_WIDTH = 512
POOL_HIST = 15
GMLP_CHUNK = 128
GMLP_WIDTH = 512
DN_HEADS = 8
DN_DK = 128
DN_DV = 128
DN_CHUNK = 64
CONV_W = 4
CONV_DIM = 3072
GROUP = 128

N_PROMPT = BATCH * SEQ
N_ROWS = N_PROMPT + DEC_BATCH

COL_QKV = 1536
COL_Z = 4608
COL_BA = 5632
COL_GATE = 5648
IN_COLS = 11792
PROJ_COLS = 12288
MIX_BLOCK = 1536
Z_WIDTH = DN_HEADS * DN_DV
GATE_SHIFT = COL_GATE - COL_BA
GATE_WIN = D_MODEL + 128

ROW_TILE = 1040
MERGE_TILE = 256
FFN_TILE = 256
PROJ_ROWS = 2080
PROJ_TILE = 1024
NORM_TILE = 512
DELTA_ROWS = 512
POOL_ROWS = 512
SAMPLE_BLOCK = 8
VMEM_LIMIT = 56 * 1024 * 1024
FFN_VMEM_LIMIT = 60 * 1024 * 1024


def _rmsnorm(x, g):
    ms = jnp.mean(x * x, axis=-1, keepdims=True)
    return x * lax.rsqrt(ms + EPS) * g


def _silu(x):
    return x * jax.nn.sigmoid(x)


def _softplus(x):
    return jnp.maximum(x, 0.0) + jnp.log1p(jnp.exp(-jnp.abs(x)))


def _bdot(a, b):
    return jnp.einsum('hmk,hkn->hmn', a.astype(BF16), b.astype(BF16),
                      preferred_element_type=F32)


def _bdot_tn(a, b):
    return jnp.einsum('hcm,hcn->hmn', a.astype(BF16), b.astype(BF16),
                      preferred_element_type=F32)


def _split(x):
    hi = x.astype(BF16)
    lo = (x - hi.astype(F32)).astype(BF16)
    return hi, lo


def _ffn_kernel(x_ref, g_ref, wg_ref, wu_ref, wd_ref, *rest):
    emit_norm = len(rest) == 4
    if emit_norm:
        gn_ref, o_ref, n_ref, h_ref = rest
    else:
        o_ref, h_ref = rest
    j = pl.program_id(1)

    @pl.when(j == 0)
    def _():
        x = x_ref[...]
        h_ref[...] = _rmsnorm(x, g_ref[...]).astype(BF16)
        o_ref[...] = x

    h = h_ref[...]
    a = jnp.dot(h, wg_ref[...].astype(BF16), preferred_element_type=F32)
    b = jnp.dot(h, wu_ref[...].astype(BF16), preferred_element_type=F32)
    act = (_silu(a) * (0.5 * b)).astype(BF16)
    o_ref[...] += jnp.dot(act, wd_ref[...].astype(BF16), preferred_element_type=F32)

    if emit_norm:
        @pl.when(j == pl.num_programs(1) - 1)
        def _():
            n_ref[...] = _rmsnorm(o_ref[...], gn_ref[...]).astype(BF16)


def _ffn(layer, x, g, wg, wu, wd, next_gain=None, tm=ROW_TILE, tf=FFN_TILE, x_buffers=2):
    rows = x.shape[0]
    vec = pl.BlockSpec((1, D_MODEL), lambda i, j: (0, 0))
    tile = pl.BlockSpec((tm, D_MODEL), lambda i, j: (i, 0))
    in_specs = [
        pl.BlockSpec((tm, D_MODEL), lambda i, j: (i, 0), pipeline_mode=pl.Buffered(x_buffers)),
        vec,
        pl.BlockSpec((None, D_MODEL, tf), lambda i, j: (layer, 0, j)),
        pl.BlockSpec((None, D_MODEL, tf), lambda i, j: (layer, 0, j)),
        pl.BlockSpec((None, tf, D_MODEL), lambda i, j: (layer, j, 0)),
    ]
    args = [x, g, wg, wu, wd]
    out_specs = [tile]
    out_shape = [jax.ShapeDtypeStruct((rows, D_MODEL), F32)]
    if next_gain is not None:
        in_specs.append(vec)
        args.append(next_gain)
        out_specs.append(pl.BlockSpec((tm, D_MODEL), lambda i, j: (i, 0)))
        out_shape.append(jax.ShapeDtypeStruct((rows, D_MODEL), BF16))
    out = pl.pallas_call(
        _ffn_kernel,
        grid=(rows // tm, FFN_DIM // tf),
        in_specs=in_specs,
        out_specs=out_specs,
        out_shape=out_shape,
        scratch_shapes=[pltpu.VMEM((tm, D_MODEL), BF16)],
        compiler_params=pltpu.CompilerParams(
            dimension_semantics=("parallel", "arbitrary"),
            vmem_limit_bytes=FFN_VMEM_LIMIT),
        name="ffn",
    )(*args)
    return out if next_gain is not None else out[0]


def _inproj_kernel(h_ref, wt_ref, o_ref):
    col = pl.program_id(1) * PROJ_TILE + lax.broadcasted_iota(jnp.int32, (PROJ_TILE, 1), 0)
    wt = jnp.where(col < IN_COLS, wt_ref[...], 0.0).astype(BF16)
    o_ref[...] = lax.dot_general(h_ref[...], wt, (((1,), (1,)), ((), ())),
                                 preferred_element_type=F32)


def _inproj(layer, h, wt):
    rows = h.shape[0]
    tm = PROJ_ROWS
    return pl.pallas_call(
        _inproj_kernel,
        grid=(rows // tm, PROJ_COLS // PROJ_TILE),
        in_specs=[
            pl.BlockSpec((tm, D_MODEL), lambda i, j: (i, 0)),
            pl.BlockSpec((None, PROJ_TILE, D_MODEL), lambda i, j: (layer, j, 0)),
        ],
        out_specs=pl.BlockSpec((tm, PROJ_TILE), lambda i, j: (i, j)),
        out_shape=jax.ShapeDtypeStruct((rows, PROJ_COLS), F32),
        compiler_params=pltpu.CompilerParams(
            dimension_semantics=("parallel", "arbitrary"),
            vmem_limit_bytes=FFN_VMEM_LIMIT),
        name="inproj",
    )(h, wt)


def _gmlp_gate(gu, gv, norm_g):
    u = jax.nn.gelu(gu)
    zv = _rmsnorm(jax.nn.gelu(gv), norm_g)
    return u, zv


def _pool_gmlp_kernel(m_ref, pw_ref, ps_ref, gn_ref, ws_ref, gb_ref,
                      ya_ref, yb_ref, ext_ref):
    c = pl.program_id(1)
    hist = 16
    rows = POOL_ROWS

    @pl.when(c == 0)
    def _():
        ext_ref[0:hist, :] = jnp.zeros((hist, POOL_WIDTH), F32)

    a = m_ref[:, 0:POOL_WIDTH]
    ext_ref[hist:hist + rows, :] = a
    pos = c * rows + lax.broadcasted_iota(jnp.int32, (rows, 1), 0)
    for gi, w in enumerate(POOL_WINDOWS):
        sl = slice(gi * GROUP, (gi + 1) * GROUP)
        s = a[:, sl]
        for i in range(1, w):
            s = s + ext_ref[hist - i:hist - i + rows, sl]
        cnt = jnp.minimum(pos + 1, w).astype(F32)
        d = s / cnt - a[:, sl]
        y = jnp.dot(d.astype(BF16), pw_ref[gi], preferred_element_type=F32)
        ya_ref[:, sl] = y * ps_ref[:, sl]
    ext_ref[0:hist, :] = a[rows - hist:rows, :]

    u, zv = _gmlp_gate(m_ref[:, POOL_WIDTH:POOL_WIDTH + GMLP_WIDTH],
                       m_ref[:, POOL_WIDTH + GMLP_WIDTH:MIX_BLOCK], gn_ref[...])
    for ch in range(rows // GMLP_CHUNK):
        r = slice(ch * GMLP_CHUNK, (ch + 1) * GMLP_CHUNK)
        for gi in range(4):
            sl = slice(gi * GROUP, (gi + 1) * GROUP)
            s = jnp.dot(ws_ref[gi], zv[r, sl].astype(BF16), preferred_element_type=F32)
            yb_ref[r, sl] = u[r, sl] * (s + gb_ref[:, sl])


def _pool_gmlp(mix, pool_w, pool_scale, gmlp_norm, ws_tril, gb_full):
    chunks = SEQ // POOL_ROWS
    row = lambda b, c: b * chunks + c
    full = lambda shape: pl.BlockSpec(shape, lambda b, c: (0,) * len(shape))
    return pl.pallas_call(
        _pool_gmlp_kernel,
        grid=(BATCH, chunks),
        in_specs=[
            pl.BlockSpec((POOL_ROWS, MIX_BLOCK), lambda b, c: (row(b, c), 0)),
            full((4, GROUP, GROUP)),
            full((1, POOL_WIDTH)),
            full((1, GMLP_WIDTH)),
            full((4, GMLP_CHUNK, GMLP_CHUNK)),
            full((GMLP_CHUNK, GMLP_WIDTH)),
        ],
        out_specs=[
            pl.BlockSpec((POOL_ROWS, POOL_WIDTH), lambda b, c: (row(b, c), 0)),
            pl.BlockSpec((POOL_ROWS, GMLP_WIDTH), lambda b, c: (row(b, c), 0)),
        ],
        out_shape=[
            jax.ShapeDtypeStruct((N_PROMPT, POOL_WIDTH), F32),
            jax.ShapeDtypeStruct((N_PROMPT, GMLP_WIDTH), F32),
        ],
        scratch_shapes=[pltpu.VMEM((16 + POOL_ROWS, POOL_WIDTH), F32)],
        compiler_params=pltpu.CompilerParams(
            dimension_semantics=("parallel", "arbitrary"),
            vmem_limit_bytes=VMEM_LIMIT),
        name="pool_gmlp",
    )(mix, pool_w, pool_scale, gmlp_norm, ws_tril, gb_full)


def _pair_bd(y, left):
    zero = jnp.zeros_like(y)
    return jnp.concatenate([jnp.where(left, y, zero), jnp.where(left, zero, y)],
                           axis=1).astype(BF16)


def _pair_mm(x, y_bd):
    return _bdot(x, y_bd)


def _pair_inverse(m, left, blk, eye):
    d = jnp.where(blk, m, 0.0)
    low = m - d
    d2 = _pair_mm(d, _pair_bd(d, left))
    d2_bd = _pair_bd(d2, left)
    d4 = _pair_mm(d2, d2_bd)
    d4_bd = _pair_bd(d4, left)
    d8_bd = _pair_bd(_pair_mm(d4, d4_bd), left)
    td = eye - d
    td = td + _pair_mm(td, d2_bd)
    td = td + _pair_mm(td, d4_bd)
    td = td + _pair_mm(td, d8_bd)
    n = _pair_mm(td, _pair_bd(low, left))
    n2 = _pair_mm(n, _pair_bd(n, left))
    x = eye - n + n2 - _pair_mm(n, _pair_bd(n2, left))
    t0 = _pair_mm(x, _pair_bd(td, left))
    th, tl = _split(t0)
    mh, ml = _split(m)
    th_bd = _pair_bd(th, left)
    tl_bd = _pair_bd(tl, left)
    mt = _bdot(mh, th_bd) + (_bdot(mh, tl_bd) + _bdot(ml, th_bd))
    r = eye - t0 - mt
    return t0 + _bdot(th, _pair_bd(r, left))


def _cumsum_exact(g, tri, dims):
    g1 = g.astype(BF16)
    r1 = g - g1.astype(F32)
    g2 = r1.astype(BF16)
    g3 = (r1 - g2.astype(F32)).astype(BF16)
    if dims == 'rows':
        dot = lambda t: jnp.dot(tri, t, preferred_element_type=F32)
    else:
        dot = lambda t: jnp.dot(t, tri, preferred_element_type=F32)
    return dot(g1) + (dot(g2) + dot(g3))


def _delta_kernel(qk_ref, kv_ref, zb_ref, bat_ref, cw_ref, acol_ref, dtcol_ref,
                  arow_ref, dtrow_ref, on_ref, yc_ref, s_ref, ext_ref):
    n = pl.program_id(1)
    C = DN_CHUNK
    H = DN_HEADS
    R = DELTA_ROWS

    @pl.when(n == 0)
    def _():
        ext_ref[0:8, :] = jnp.zeros((8, CONV_DIM), F32)
        s_ref[...] = jnp.zeros(s_ref.shape, F32)

    ext_ref[8:8 + R, 0:MIX_BLOCK] = qk_ref[...]
    ext_ref[8:8 + R, MIX_BLOCK:CONV_DIM] = kv_ref[...]
    conv = cw_ref[3:4, :] * ext_ref[8:8 + R, :]
    for i in range(CONV_W - 1):
        conv = conv + cw_ref[i:i + 1, :] * ext_ref[5 + i:5 + i + R, :]
    ext_ref[0:8, :] = ext_ref[R:R + 8, :]
    act = _silu(conv)

    ri = lax.broadcasted_iota(jnp.int32, (C, 2 * C), 0)
    li = lax.broadcasted_iota(jnp.int32, (C, 2 * C), 1)
    ci = li % C
    left = li < C
    causal = ri >= ci
    strict = ri > ci
    blk = (ri // 16) == (ci // 16)
    eye = jnp.where(ri == ci, 1.0, 0.0).astype(F32)
    zero = jnp.zeros((C, DN_DK), F32)

    rr = lax.broadcasted_iota(jnp.int32, (R, R), 0)
    cc = lax.broadcasted_iota(jnp.int32, (R, R), 1)
    tril_chunks = jnp.where((rr // C == cc // C) & (rr >= cc), 1.0, 0.0).astype(BF16)
    pr = lax.broadcasted_iota(jnp.int32, (2 * C, 2 * C), 0)
    pc = lax.broadcasted_iota(jnp.int32, (2 * C, 2 * C), 1)
    triu_pair = jnp.where((pr // C == pc // C) & (pr <= pc), 1.0, 0.0).astype(BF16)

    ba = zb_ref[:, Z_WIDTH:Z_WIDTH + 128]
    beta_all = jax.nn.sigmoid(ba)
    g_all = -jnp.exp(acol_ref[...]) * _softplus(ba + dtcol_ref[...])
    gc_all = _cumsum_exact(g_all, tril_chunks, 'rows')
    on = on_ref[...]

    NC = R // C
    NP = H // 2
    lane2 = lambda a, b: jnp.concatenate([a, b], axis=1)
    diag2 = lambda a, b: jnp.concatenate([lane2(a, jnp.zeros_like(b)),
                                          lane2(jnp.zeros_like(a), b)], axis=0)

    ks, kbs, qes, gcs = [], [], [], []
    kq_lhs, kq_rhs, gc_pair, gr_pair, uw_rhs = [], [], [], [], []
    for c in range(NC):
        rows = slice(c * C, (c + 1) * C)
        heads = lambda base: jnp.stack(
            [act[rows, (base + h) * DN_DK:(base + h + 1) * DN_DK] for h in range(H)])
        q = heads(0)
        k = heads(H)
        v = heads(2 * H)
        q = q * lax.rsqrt(jnp.sum(q * q, axis=-1, keepdims=True) + EPS) * (DN_DK ** -0.5)
        k = k * lax.rsqrt(jnp.sum(k * k, axis=-1, keepdims=True) + EPS)
        beta = jnp.stack([beta_all[rows, h:h + 1] for h in range(H)])
        gc = jnp.stack([gc_all[rows, H + h:H + h + 1] for h in range(H)])
        egc = jnp.exp(gc)
        kb = k * beta
        vb = v * beta
        kbe = kb * egc
        ks.append(k)
        kbs.append(kb)
        qes.append(q * egc)
        gcs.append(gc)
        g_t = -jnp.exp(arow_ref[...]) * _softplus(bat_ref[c] + dtrow_ref[...])
        gr_all = _cumsum_exact(g_t, triu_pair, 'cols')
        for p in range(NP):
            a, b = 2 * p, 2 * p + 1
            kq_lhs.append(jnp.concatenate([lane2(kb[a], kb[b]), lane2(q[a], q[b])], axis=0))
            kq_rhs.append(diag2(k[a], k[b]))
            gc_pair.append(jnp.where(left, gc[a], gc[b]))
            gr_pair.append(gr_all[p:p + 1, :])
            uw_rhs.append(diag2(lane2(vb[a], kbe[a]), lane2(vb[b], kbe[b])))

    kq = jnp.einsum('bmk,bnk->bmn', jnp.stack(kq_lhs).astype(BF16), jnp.stack(kq_rhs).astype(BF16),
                    preferred_element_type=F32)
    diff = jnp.stack(gc_pair) - jnp.stack(gr_pair)
    decay = jnp.where(causal, jnp.exp(jnp.where(causal, diff, 0.0)), 0.0)
    m = jnp.where(strict, kq[:, 0:C] * decay, 0.0)
    a_mat = kq[:, C:2 * C] * decay
    t = _pair_inverse(m, left, blk, eye)
    uw = _bdot(t, jnp.stack(uw_rhs))

    for c in range(NC):
        rows = slice(c * C, (c + 1) * C)
        part = lambda h, i: uw[c * NP + h // 2][:, (2 * (h % 2) + i) * DN_DV:
                                               (2 * (h % 2) + i + 1) * DN_DV]
        u = jnp.stack([part(h, 0) for h in range(H)])
        w = jnp.stack([part(h, 1) for h in range(H)])
        s = s_ref[0]
        ws_qs = _bdot(jnp.concatenate([w, qes[c]], axis=1), s)
        v_new = u - ws_qs[:, 0:C]
        glast = gcs[c][:, C - 1:C, :]
        kg = ks[c] * jnp.exp(glast - gcs[c])
        s_ref[0] = s * jnp.exp(glast) + _bdot_tn(kg, v_new)
        vn_bd = jnp.stack([diag2(v_new[2 * p], v_new[2 * p + 1]) for p in range(NP)])
        av = _bdot(a_mat[c * NP:(c + 1) * NP], vn_bd)
        o = ws_qs[:, C:2 * C] + jnp.stack(
            [av[h // 2][:, (h % 2) * DN_DV:(h % 2 + 1) * DN_DV] for h in range(H)])
        o = o * lax.rsqrt(jnp.mean(o * o, axis=-1, keepdims=True) + EPS) * on
        for h in range(H):
            sl = slice(h * DN_DV, (h + 1) * DN_DV)
            yc_ref[rows, sl] = o[h] * _silu(zb_ref[rows, sl])


def _delta(mix, bat, conv_w, acol, dtcol, arow, dtrow, onorm):
    steps = SEQ // DELTA_ROWS
    cps = DELTA_ROWS // DN_CHUNK
    row = lambda b, n: b * steps + n
    full = lambda shape: pl.BlockSpec(shape, lambda b, n: (0,) * len(shape))
    return pl.pallas_call(
        _delta_kernel,
        grid=(BATCH, steps),
        in_specs=[
            pl.BlockSpec((DELTA_ROWS, MIX_BLOCK), lambda b, n: (row(b, n), 1)),
            pl.BlockSpec((DELTA_ROWS, MIX_BLOCK), lambda b, n: (row(b, n), 2)),
            pl.BlockSpec((DELTA_ROWS, MIX_BLOCK), lambda b, n: (row(b, n), 3)),
            pl.BlockSpec((cps, 8, 2 * DN_CHUNK), lambda b, n: (row(b, n), 0, 0)),
            full((CONV_W, CONV_DIM)),
            full((1, 128)),
            full((1, 128)),
            full((8, 2 * DN_CHUNK)),
            full((8, 2 * DN_CHUNK)),
            full((1, DN_DV)),
        ],
        out_specs=[
            pl.BlockSpec((DELTA_ROWS, Z_WIDTH), lambda b, n: (row(b, n), 0)),
            pl.BlockSpec((1, DN_HEADS, DN_DK, DN_DV), lambda b, n: (b, 0, 0, 0)),
        ],
        out_shape=[
            jax.ShapeDtypeStruct((N_PROMPT, Z_WIDTH), F32),
            jax.ShapeDtypeStruct((BATCH, DN_HEADS, DN_DK, DN_DV), F32),
        ],
        scratch_shapes=[pltpu.VMEM((8 + DELTA_ROWS, CONV_DIM), F32)],
        compiler_params=pltpu.CompilerParams(
            dimension_semantics=("parallel", "arbitrary"),
            vmem_limit_bytes=VMEM_LIMIT),
        name="delta",
    )(mix, mix, mix, bat, conv_w, acol, dtcol, arow, dtrow, onorm)


def _sample_rows_kernel(m_ref, qk_ref, kv_ref, zb_ref, pool_ref, cst_ref,
                        pw_ref, ps_ref, gn_ref, wsd_ref, gb0_ref, cw_ref, acol_ref, dtcol_ref,
                        ya_ref, yb_ref, zv_ref, npool_ref, nconv_ref, vec_ref, eg_ref, qk_out):
    H = DN_HEADS
    a = m_ref[:, 0:POOL_WIDTH]
    for gi, w in enumerate(POOL_WINDOWS):
        sl = slice(gi * GROUP, (gi + 1) * GROUP)
        s = a[:, sl]
        for i in range(1, w):
            s = s + pool_ref[POOL_HIST - i, :, sl]
        d = s / float(w) - a[:, sl]
        y = jnp.dot(d.astype(BF16), pw_ref[gi], preferred_element_type=F32)
        ya_ref[:, sl] = y * ps_ref[:, sl]
    for i in range(POOL_HIST - 1):
        npool_ref[i] = pool_ref[i + 1]
    npool_ref[POOL_HIST - 1] = a

    u, zv = _gmlp_gate(m_ref[:, POOL_WIDTH:POOL_WIDTH + GMLP_WIDTH],
                       m_ref[:, POOL_WIDTH + GMLP_WIDTH:MIX_BLOCK], gn_ref[...])
    zv_ref[...] = zv
    yb_ref[...] = u * (wsd_ref[...] * zv + gb0_ref[...])

    x = jnp.concatenate([qk_ref[...], kv_ref[...]], axis=-1)
    conv = cw_ref[3:4, :] * x
    for i in range(CONV_W - 1):
        conv = conv + cw_ref[i:i + 1, :] * cst_ref[i]
    nconv_ref[0] = cst_ref[1]
    nconv_ref[1] = cst_ref[2]
    nconv_ref[2] = x
    act = _silu(conv)
    ba = zb_ref[:, Z_WIDTH:Z_WIDTH + 128]
    beta_all = jax.nn.sigmoid(ba)
    g_all = -jnp.exp(acol_ref[...]) * _softplus(ba + dtcol_ref[...])
    eg_all = jnp.exp(g_all)
    eg_ref[...] = eg_all
    lane = lax.broadcasted_iota(jnp.int32, (DEC_BATCH, 128), 1)
    qk_all = jnp.zeros((DEC_BATCH, 128), F32)
    for h in range(H):
        q = act[:, h * DN_DK:(h + 1) * DN_DK]
        k = act[:, (H + h) * DN_DK:(H + h + 1) * DN_DK]
        v = act[:, (2 * H + h) * DN_DK:(2 * H + h + 1) * DN_DK]
        q = q * lax.rsqrt(jnp.sum(q * q, axis=-1, keepdims=True) + EPS) * (DN_DK ** -0.5)
        k = k * lax.rsqrt(jnp.sum(k * k, axis=-1, keepdims=True) + EPS)
        beta = beta_all[:, h:h + 1]
        eg = eg_all[:, H + h:H + h + 1]
        vec_ref[:, h * 128:(h + 1) * 128] = k * (beta * eg)
        vec_ref[:, (H + h) * 128:(H + h + 1) * 128] = q * eg
        vec_ref[:, (2 * H + h) * 128:(2 * H + h + 1) * 128] = k
        vec_ref[:, (3 * H + h) * 128:(3 * H + h + 1) * 128] = v * beta
        qk = jnp.sum(q * k, axis=-1, keepdims=True)
        qk_all = jnp.where(lane == h, qk, qk_all)
    qk_out[...] = qk_all


def _sample_rows(mix, pool_t, conv_t, pool_w, pool_scale, gmlp_norm, ws_diag, gb0, conv_w,
                 acol, dtcol):
    rb = N_PROMPT // DEC_BATCH
    full = lambda shape: pl.BlockSpec(shape, lambda i: (0,) * len(shape))
    n = DEC_BATCH
    return pl.pallas_call(
        _sample_rows_kernel,
        grid=(1,),
        in_specs=[
            pl.BlockSpec((n, MIX_BLOCK), lambda i: (rb, 0)),
            pl.BlockSpec((n, MIX_BLOCK), lambda i: (rb, 1)),
            pl.BlockSpec((n, MIX_BLOCK), lambda i: (rb, 2)),
            pl.BlockSpec((n, MIX_BLOCK), lambda i: (rb, 3)),
            full((POOL_HIST, n, POOL_WIDTH)),
            full((CONV_W - 1, n, CONV_DIM)),
            full((4, GROUP, GROUP)),
            full((1, POOL_WIDTH)),
            full((1, GMLP_WIDTH)),
            full((1, GMLP_WIDTH)),
            full((1, GMLP_WIDTH)),
            full((CONV_W, CONV_DIM)),
            full((1, 128)),
            full((1, 128)),
        ],
        out_specs=[
            full((n, POOL_WIDTH)),
            full((n, GMLP_WIDTH)),
            full((n, GMLP_WIDTH)),
            full((POOL_HIST, n, POOL_WIDTH)),
            full((CONV_W - 1, n, CONV_DIM)),
            full((n, 4 * DN_HEADS * 128)),
            full((n, 128)),
            full((n, 128)),
        ],
        out_shape=[
            jax.ShapeDtypeStruct((n, POOL_WIDTH), F32),
            jax.ShapeDtypeStruct((n, GMLP_WIDTH), F32),
            jax.ShapeDtypeStruct((n, GMLP_WIDTH), F32),
            jax.ShapeDtypeStruct((POOL_HIST, n, POOL_WIDTH), F32),
            jax.ShapeDtypeStruct((CONV_W - 1, n, CONV_DIM), F32),
            jax.ShapeDtypeStruct((n, 4 * DN_HEADS * 128), F32),
            jax.ShapeDtypeStruct((n, 128), F32),
            jax.ShapeDtypeStruct((n, 128), F32),
        ],
        compiler_params=pltpu.CompilerParams(
            dimension_semantics=("arbitrary",),
            vmem_limit_bytes=VMEM_LIMIT),
        name="sample_rows",
    )(mix, mix, mix, mix, pool_t, conv_t, pool_w, pool_scale, gmlp_norm,
      ws_diag, gb0, conv_w, acol, dtcol)


def _sample_state_kernel(vec_ref, scal_ref, z_ref, on_ref, s_ref, *rest):
    yc_ref, so_ref = rest[-2:]
    H = DN_HEADS
    row = lax.broadcasted_iota(jnp.int32, (1, 8, 1), 1)
    for h in range(H):
        s = s_ref[0, :, h]
        w = vec_ref[:, h:h + 1, :]
        qg = vec_ref[:, H + h:H + h + 1, :]
        k = vec_ref[:, 2 * H + h:2 * H + h + 1, :]
        vb = vec_ref[:, 3 * H + h:3 * H + h + 1, :]
        eg = scal_ref[:, h:h + 1, :]
        qk = scal_ref[:, H + h:H + h + 1, :]
        lhs = jnp.where(row == 0, w, jnp.where(row == 1, qg, 0.0))
        r = _bdot(lhs, s)
        v_new = vb - r[:, 0:1, :]
        o = r[:, 1:2, :] + qk * v_new
        k8 = jnp.where(row == 0, k, 0.0)
        v8 = jnp.broadcast_to(v_new, k8.shape)
        so_ref[0, :, h] = s * eg + _bdot_tn(k8, v8)
        o = o * lax.rsqrt(jnp.mean(o * o, axis=-1, keepdims=True) + EPS) * on_ref[...]
        yc_ref[:, h:h + 1, :] = o * _silu(z_ref[:, h:h + 1, :])


def _sample_state(layer, vec3, scal3, z3, onorm, state, stacked):
    sb = SAMPLE_BLOCK
    H = DN_HEADS
    in_specs = [
        pl.BlockSpec((sb, 4 * H, 128), lambda i: (i, 0, 0)),
        pl.BlockSpec((sb, 2 * H, 1), lambda i: (i, 0, 0)),
        pl.BlockSpec((sb, H, DN_DV), lambda i: (i, 0, 0)),
        pl.BlockSpec((1, DN_DV), lambda i: (0, 0)),
        pl.BlockSpec((1, sb, H, DN_DK, DN_DV), lambda i: (layer, i, 0, 0, 0)),
    ]
    args = [vec3, scal3, z3, onorm, state]
    aliases = {}
    if stacked is not None:
        in_specs.append(pl.BlockSpec(memory_space=pl.ANY))
        args.append(stacked)
        aliases = {5: 1}
    return pl.pallas_call(
        _sample_state_kernel,
        grid=(DEC_BATCH // sb,),
        in_specs=in_specs,
        out_specs=[
            pl.BlockSpec((sb, H, DN_DV), lambda i: (i, 0, 0)),
            pl.BlockSpec((1, sb, H, DN_DK, DN_DV), lambda i: (layer, i, 0, 0, 0)),
        ],
        out_shape=[
            jax.ShapeDtypeStruct((DEC_BATCH, H, DN_DV), F32),
            jax.ShapeDtypeStruct((DEPTH, DEC_BATCH, H, DN_DK, DN_DV), F32),
        ],
        input_output_aliases=aliases,
        compiler_params=pltpu.CompilerParams(
            dimension_semantics=("parallel",),
            vmem_limit_bytes=VMEM_LIMIT),
        name="sample_state",
    )(*args)


def _merge_kernel(x_ref, g0_ref, g1_ref, g2_ref, bg_ref, ya_ref, yb_ref, yc_ref,
                  pa_ref, pb_ref, pc_ref, wo_ref, *rest):
    o_ref = rest[-1]
    lane = lax.broadcasted_iota(jnp.int32, (1, GATE_WIN), 1)
    valid = (lane >= GATE_SHIFT) & (lane < GATE_SHIFT + D_MODEL)

    def branch(g_ref, i, y_ref, p_ref):
        gate = jax.nn.sigmoid(g_ref[...] + bg_ref[:, i * GATE_WIN:(i + 1) * GATE_WIN])
        gate = jnp.where(valid, gate, 0.0)
        return gate * jnp.dot(y_ref[...].astype(BF16), p_ref[...], preferred_element_type=F32)

    m = branch(g0_ref, 0, ya_ref, pa_ref)
    m = m + branch(g1_ref, 1, yb_ref, pb_ref)
    m = m + branch(g2_ref, 2, yc_ref, pc_ref)
    o_ref[...] = x_ref[...] + jnp.dot(m.astype(BF16), wo_ref[...], preferred_element_type=F32)


def _shift_cols_kernel(w_ref, o_ref):
    w = w_ref[...]
    wide = jnp.concatenate([w, jnp.zeros((w.shape[0], GATE_WIN - D_MODEL), F32)], axis=1)
    o_ref[...] = pltpu.roll(wide, GATE_SHIFT, axis=1).astype(BF16)


def _shift_cols(w):
    depth, k, _ = w.shape
    rows = 512
    return pl.pallas_call(
        _shift_cols_kernel,
        grid=(depth, k // rows),
        in_specs=[pl.BlockSpec((None, rows, D_MODEL), lambda l, r: (l, r, 0))],
        out_specs=pl.BlockSpec((None, rows, GATE_WIN), lambda l, r: (l, r, 0)),
        out_shape=jax.ShapeDtypeStruct((depth, k, GATE_WIN), BF16),
        compiler_params=pltpu.CompilerParams(dimension_semantics=("parallel", "parallel"),
                                             vmem_limit_bytes=VMEM_LIMIT),
        name="shift_cols",
    )(w)


def _shift_rows_kernel(w_ref, o_ref):
    tail = GATE_WIN - D_MODEL - GATE_SHIFT
    o_ref[0:GATE_SHIFT, :] = jnp.zeros((GATE_SHIFT, D_MODEL), BF16)
    o_ref[GATE_SHIFT:GATE_SHIFT + D_MODEL, :] = w_ref[...].astype(BF16)
    o_ref[GATE_SHIFT + D_MODEL:GATE_WIN, :] = jnp.zeros((tail, D_MODEL), BF16)


def _shift_rows(w):
    depth = w.shape[0]
    return pl.pallas_call(
        _shift_rows_kernel,
        grid=(depth,),
        in_specs=[pl.BlockSpec((None, D_MODEL, D_MODEL), lambda l: (l, 0, 0))],
        out_specs=pl.BlockSpec((None, GATE_WIN, D_MODEL), lambda l: (l, 0, 0)),
        out_shape=jax.ShapeDtypeStruct((depth, GATE_WIN, D_MODEL), BF16),
        compiler_params=pltpu.CompilerParams(dimension_semantics=("parallel",),
                                             vmem_limit_bytes=VMEM_LIMIT),
        name="shift_rows",
    )(w)


def _merge(layer, x, proj, b_gate, ya, yb, yc, pa, pb, pc, wo, tile, first, count, prev):
    once = lambda shape: pl.BlockSpec(shape, lambda i: (0,) * len(shape),
                                      pipeline_mode=pl.Buffered(1))
    slab = lambda k, n: pl.BlockSpec((None, k, n), lambda i: (layer, 0, 0),
                                     pipeline_mode=pl.Buffered(1))
    rows = lambda width: pl.BlockSpec((tile, width), lambda i: (first + i, 0))
    local = lambda width: pl.BlockSpec((tile, width), lambda i: (i, 0))
    gate = lambda k: pl.BlockSpec((pl.Element(tile), pl.Element(GATE_WIN)),
                                  lambda i: ((first + i) * tile, COL_BA + k * D_MODEL))
    in_specs = [
        rows(D_MODEL),
        gate(0), gate(1), gate(2),
        once((1, 3 * GATE_WIN)),
        local(POOL_WIDTH), local(GMLP_WIDTH), local(Z_WIDTH),
        slab(POOL_WIDTH, GATE_WIN),
        slab(GMLP_WIDTH, GATE_WIN),
        slab(Z_WIDTH, GATE_WIN),
        slab(GATE_WIN, D_MODEL),
    ]
    args = [x, proj, proj, proj, b_gate, ya, yb, yc, pa, pb, pc, wo]
    aliases = {}
    if prev is not None:
        in_specs.append(pl.BlockSpec(memory_space=pl.ANY))
        args.append(prev)
        aliases = {len(args) - 1: 0}
    return pl.pallas_call(
        _merge_kernel,
        grid=(count,),
        in_specs=in_specs,
        out_specs=rows(D_MODEL),
        out_shape=jax.ShapeDtypeStruct((N_ROWS, D_MODEL), F32),
        input_output_aliases=aliases,
        compiler_params=pltpu.CompilerParams(
            dimension_semantics=("parallel",),
            vmem_limit_bytes=VMEM_LIMIT),
        name="merge",
    )(*args)


def _final_kernel(x_ref, g_ref, o_ref):
    o_ref[...] = _rmsnorm(x_ref[...], g_ref[...])


def _final_norm(x, g, t, first, count):
    return pl.pallas_call(
        _final_kernel,
        grid=(count,),
        in_specs=[pl.BlockSpec((t, D_MODEL), lambda i: (first + i, 0)),
                  pl.BlockSpec((1, D_MODEL), lambda i: (0, 0))],
        out_specs=pl.BlockSpec((t, D_MODEL), lambda i: (i, 0)),
        out_shape=jax.ShapeDtypeStruct((count * t, D_MODEL), F32),
        compiler_params=pltpu.CompilerParams(dimension_semantics=("parallel",),
                                             vmem_limit_bytes=VMEM_LIMIT),
        name="final_norm",
    )(x, g)


def _lane_pad(v, offset, width=128):
    return jnp.zeros((1, width), F32).at[0, offset:offset + v.shape[0]].set(v)


def _tail_rows(mix, count, col, width):
    return jnp.stack([
        lax.slice(mix, ((b + 1) * SEQ - count, col), ((b + 1) * SEQ, col + width))
        for b in range(BATCH)])


def kernel(x_prompt, x_sample, state_delta, state_conv, state_pool, ffn1_norm, ffn1_wg, ffn1_wu, ffn1_wd, mix_norm, w_in, b_gate, pool_w, pool_scale, gmlp_norm, gmlp_ws, gmlp_b, dn_conv, dn_a_log, dn_dt_bias, dn_onorm, proj_a, proj_b, proj_c, w_o, ffn2_norm, ffn2_wg, ffn2_wu, ffn2_wd, final_norm):
    H = DN_HEADS
    x = jnp.concatenate([x_prompt.reshape(N_PROMPT, D_MODEL),
                         x_sample.reshape(DEC_BATCH, D_MODEL)], axis=0)
    causal = jnp.tril(jnp.ones((GMLP_CHUNK, GMLP_CHUNK), bool))
    outs_delta_p, outs_conv_p, outs_conv_s = [], [], []
    outs_pool_p, outs_pool_s, outs_zv = [], [], []
    delta_s = None
    w_in_t = jnp.swapaxes(w_in, 1, 2)
    weights = (_shift_cols(proj_a), _shift_cols(proj_b), _shift_cols(proj_c), _shift_rows(w_o))
    shift_cols = ((0, 0), (GATE_SHIFT, GATE_WIN - D_MODEL - GATE_SHIFT))
    for l in range(DEPTH):
        x, h_mix = _ffn(l, x, ffn1_norm[l][None], ffn1_wg, ffn1_wu, ffn1_wd,
                        next_gain=mix_norm[l][None])

        mix = _inproj(l, h_mix, w_in_t)

        pw = pool_w[l].astype(BF16)
        ps = pool_scale[l][None]
        gn = gmlp_norm[l][None]
        ws_tril = jnp.where(causal, gmlp_ws[l], 0).astype(BF16)
        gb_full = jnp.repeat(gmlp_b[l].T, GROUP, axis=1)
        ws_diag = jnp.repeat(gmlp_ws[l][:, 0, 0], GROUP)[None]
        gb0 = jnp.repeat(gmlp_b[l][:, 0], GROUP)[None]
        acol = _lane_pad(dn_a_log[l], H)
        dtcol = _lane_pad(dn_dt_bias[l], H)
        pair_rows = lambda v: jnp.pad(jnp.repeat(v.reshape(H // 2, 2), DN_CHUNK, axis=1),
                                      ((0, 8 - H // 2), (0, 0)))
        arow = pair_rows(dn_a_log[l])
        dtrow = pair_rows(dn_dt_bias[l])
        onorm = dn_onorm[l][None]

        ya_p, yb_p = _pool_gmlp(mix, pw, ps, gn, ws_tril, gb_full)
        a_p = lax.slice(mix, (0, COL_BA + H), (N_PROMPT, COL_BA + 2 * H))
        bat = a_p.reshape(N_PROMPT // DN_CHUNK, DN_CHUNK, H // 2, 2).transpose(0, 2, 3, 1)
        bat = jnp.pad(bat.reshape(N_PROMPT // DN_CHUNK, H // 2, 2 * DN_CHUNK),
                      ((0, 0), (0, 8 - H // 2), (0, 0)))
        yc_p, s_p = _delta(mix, bat, dn_conv[l], acol, dtcol, arow, dtrow, onorm)
        outs_delta_p.append(s_p)
        outs_conv_p.append(_tail_rows(mix, CONV_W - 1, COL_QKV, CONV_DIM))
        outs_pool_p.append(_tail_rows(mix, POOL_HIST, 0, POOL_WIDTH))

        pool_t = state_pool[l].transpose(1, 0, 2)
        conv_t = state_conv[l].transpose(1, 0, 2)
        ya_s, yb_s, zv_s, npool_t, nconv_t, vec, eg_all, qk_all = _sample_rows(
            mix, pool_t, conv_t, pw, ps, gn, ws_diag, gb0, dn_conv[l], acol, dtcol)
        vec3 = vec.reshape(DEC_BATCH, 4 * H, 128)
        scal3 = jnp.concatenate([eg_all[:, H:2 * H], qk_all[:, :H]], axis=1)[:, :, None]
        z3 = lax.slice(mix, (N_PROMPT, COL_Z), (N_ROWS, COL_Z + Z_WIDTH)).reshape(DEC_BATCH, H, DN_DV)
        yc_s3, delta_s = _sample_state(l, vec3, scal3, z3, onorm, state_delta, delta_s)
        outs_conv_s.append(nconv_t.transpose(1, 0, 2))
        outs_pool_s.append(npool_t.transpose(1, 0, 2))
        outs_zv.append(zv_s[:, None, :])

        bg = jnp.pad(b_gate[l].reshape(3, D_MODEL), shift_cols).reshape(1, 3 * GATE_WIN)
        x_new = _merge(l, x, mix, bg, ya_p, yb_p, yc_p, *weights,
                       tile=MERGE_TILE, first=0, count=N_PROMPT // MERGE_TILE, prev=None)
        x = _merge(l, x, mix, bg, ya_s, yb_s, yc_s3.reshape(DEC_BATCH, Z_WIDTH), *weights,
                   tile=DEC_BATCH, first=N_PROMPT // DEC_BATCH, count=1, prev=x_new)

        x = _ffn(l, x, ffn2_norm[l][None], ffn2_wg, ffn2_wu, ffn2_wd)

    g = final_norm[None]
    y_prompt = _final_norm(x, g, NORM_TILE, 0, N_PROMPT // NORM_TILE).reshape(BATCH, SEQ, D_MODEL)
    y_sample = _final_norm(x, g, DEC_BATCH, N_PROMPT // DEC_BATCH, 1).reshape(
        DEC_BATCH, 1, D_MODEL)
    return (y_prompt, y_sample, jnp.stack(outs_delta_p), delta_s,
            jnp.stack(outs_conv_p), jnp.stack(outs_conv_s), jnp.stack(outs_pool_p),
            jnp.stack(outs_pool_s), jnp.stack(outs_zv))
```

```python
import jax
import jax.numpy as jnp
from jax import lax
from jax.experimental import pallas as pl
from jax.experimental.pallas import tpu as pltpu

F32 = jnp.float32
BF16 = jnp.bfloat16

D_MODEL = 2048
BATCH = 4
SEQ = 2048
DEPTH = 4
DEC_BATCH = 128
EPS = 1e-6
FFN_DIM = 5632
POOL_WINDOWS = (2, 4, 8, 16)
POOL_WIDTH = 512
POOL_HIST = 15
GMLP_CHUNK = 128
GMLP_WIDTH = 512
DN_HEADS = 8
DN_DK = 128
DN_DV = 128
DN_CHUNK = 64
CONV_W = 4
CONV_DIM = 3072
GROUP = 128

N_PROMPT = BATCH * SEQ
N_ROWS = N_PROMPT + DEC_BATCH

COL_QKV = 1536
COL_Z = 4608
COL_BA = 5632
COL_GATE = 5648
IN_COLS = 11792
PROJ_COLS = 12288
MIX_BLOCK = 1536
Z_WIDTH = DN_HEADS * DN_DV
GATE_SHIFT = COL_GATE - COL_BA
GATE_WIN = D_MODEL + 128

ROW_TILE = 1040
MERGE_TILE = 256
FFN_TILE = 256
PROJ_ROWS = 2080
PROJ_TILE = 1024
NORM_TILE = 512
DELTA_ROWS = 512
POOL_ROWS = 512
SAMPLE_BLOCK = 8
VMEM_LIMIT = 56 * 1024 * 1024
FFN_VMEM_LIMIT = 60 * 1024 * 1024


def _rmsnorm(x, g):
    ms = jnp.mean(x * x, axis=-1, keepdims=True)
    return x * lax.rsqrt(ms + EPS) * g


def _silu(x):
    return x * jax.nn.sigmoid(x)


def _softplus(x):
    return jnp.maximum(x, 0.0) + jnp.log1p(jnp.exp(-jnp.abs(x)))


def _bdot(a, b):
    return jnp.einsum('hmk,hkn->hmn', a.astype(BF16), b.astype(BF16),
                      preferred_element_type=F32)


def _bdot_tn(a, b):
    return jnp.einsum('hcm,hcn->hmn', a.astype(BF16), b.astype(BF16),
                      preferred_element_type=F32)


def _split(x):
    hi = x.astype(BF16)
    lo = (x - hi.astype(F32)).astype(BF16)
    return hi, lo


def _ffn_kernel(x_ref, g_ref, wg_ref, wu_ref, wd_ref, *rest):
    emit_norm = len(rest) == 4
    if emit_norm:
        gn_ref, o_ref, n_ref, h_ref = rest
    else:
        o_ref, h_ref = rest
    j = pl.program_id(1)

    @pl.when(j == 0)
    def _():
        x = x_ref[...]
        h_ref[...] = _rmsnorm(x, g_ref[...]).astype(BF16)
        o_ref[...] = x

    h = h_ref[...]
    a = jnp.dot(h, wg_ref[...].astype(BF16), preferred_element_type=F32)
    b = jnp.dot(h, wu_ref[...].astype(BF16), preferred_element_type=F32)
    act = (_silu(a) * (0.5 * b)).astype(BF16)
    o_ref[...] += jnp.dot(act, wd_ref[...].astype(BF16), preferred_element_type=F32)

    if emit_norm:
        @pl.when(j == pl.num_programs(1) - 1)
        def _():
            n_ref[...] = _rmsnorm(o_ref[...], gn_ref[...]).astype(BF16)


def _ffn(layer, x, g, wg, wu, wd, next_gain=None, tm=ROW_TILE, tf=FFN_TILE, x_buffers=2):
    rows = x.shape[0]
    vec = pl.BlockSpec((1, D_MODEL), lambda i, j: (0, 0))
    tile = pl.BlockSpec((tm, D_MODEL), lambda i, j: (i, 0))
    in_specs = [
        pl.BlockSpec((tm, D_MODEL), lambda i, j: (i, 0), pipeline_mode=pl.Buffered(x_buffers)),
        vec,
        pl.BlockSpec((None, D_MODEL, tf), lambda i, j: (layer, 0, j)),
        pl.BlockSpec((None, D_MODEL, tf), lambda i, j: (layer, 0, j)),
        pl.BlockSpec((None, tf, D_MODEL), lambda i, j: (layer, j, 0)),
    ]
    args = [x, g, wg, wu, wd]
    out_specs = [tile]
    out_shape = [jax.ShapeDtypeStruct((rows, D_MODEL), F32)]
    if next_gain is not None:
        in_specs.append(vec)
        args.append(next_gain)
        out_specs.append(pl.BlockSpec((tm, D_MODEL), lambda i, j: (i, 0)))
        out_shape.append(jax.ShapeDtypeStruct((rows, D_MODEL), BF16))
    out = pl.pallas_call(
        _ffn_kernel,
        grid=(rows // tm, FFN_DIM // tf),
        in_specs=in_specs,
        out_specs=out_specs,
        out_shape=out_shape,
        scratch_shapes=[pltpu.VMEM((tm, D_MODEL), BF16)],
        compiler_params=pltpu.CompilerParams(
            dimension_semantics=("parallel", "arbitrary"),
            vmem_limit_bytes=FFN_VMEM_LIMIT),
        name="ffn",
    )(*args)
    return out if next_gain is not None else out[0]


def _inproj_kernel(h_ref, wt_ref, o_ref):
    col = pl.program_id(1) * PROJ_TILE + lax.broadcasted_iota(jnp.int32, (PROJ_TILE, 1), 0)
    wt = jnp.where(col < IN_COLS, wt_ref[...], 0.0).astype(BF16)
    o_ref[...] = lax.dot_general(h_ref[...], wt, (((1,), (1,)), ((), ())),
                                 preferred_element_type=F32)


def _inproj(layer, h, wt):
    rows = h.shape[0]
    tm = PROJ_ROWS
    return pl.pallas_call(
        _inproj_kernel,
        grid=(rows // tm, PROJ_COLS // PROJ_TILE),
        in_specs=[
            pl.BlockSpec((tm, D_MODEL), lambda i, j: (i, 0)),
            pl.BlockSpec((None, PROJ_TILE, D_MODEL), lambda i, j: (layer, j, 0)),
        ],
        out_specs=pl.BlockSpec((tm, PROJ_TILE), lambda i, j: (i, j)),
        out_shape=jax.ShapeDtypeStruct((rows, PROJ_COLS), F32),
        compiler_params=pltpu.CompilerParams(
            dimension_semantics=("parallel", "arbitrary"),
            vmem_limit_bytes=FFN_VMEM_LIMIT),
        name="inproj",
    )(h, wt)


def _gmlp_gate(gu, gv, norm_g):
    u = jax.nn.gelu(gu)
    zv = _rmsnorm(jax.nn.gelu(gv), norm_g)
    return u, zv


def _pool_gmlp_kernel(m_ref, pw_ref, ps_ref, gn_ref, ws_ref, gb_ref,
                      ya_ref, yb_ref, ext_ref):
    c = pl.program_id(1)
    hist = 16
    rows = POOL_ROWS

    @pl.when(c == 0)
    def _():
        ext_ref[0:hist, :] = jnp.zeros((hist, POOL_WIDTH), F32)

    a = m_ref[:, 0:POOL_WIDTH]
    ext_ref[hist:hist + rows, :] = a
    pos = c * rows + lax.broadcasted_iota(jnp.int32, (rows, 1), 0)
    for gi, w in enumerate(POOL_WINDOWS):
        sl = slice(gi * GROUP, (gi + 1) * GROUP)
        s = a[:, sl]
        for i in range(1, w):
            s = s + ext_ref[hist - i:hist - i + rows, sl]
        cnt = jnp.minimum(pos + 1, w).astype(F32)
        d = s / cnt - a[:, sl]
        y = jnp.dot(d.astype(BF16), pw_ref[gi], preferred_element_type=F32)
        ya_ref[:, sl] = y * ps_ref[:, sl]
    ext_ref[0:hist, :] = a[rows - hist:rows, :]

    u, zv = _gmlp_gate(m_ref[:, POOL_WIDTH:POOL_WIDTH + GMLP_WIDTH],
                       m_ref[:, POOL_WIDTH + GMLP_WIDTH:MIX_BLOCK], gn_ref[...])
    for ch in range(rows // GMLP_CHUNK):
        r = slice(ch * GMLP_CHUNK, (ch + 1) * GMLP_CHUNK)
        for gi in range(4):
            sl = slice(gi * GROUP, (gi + 1) * GROUP)
            s = jnp.dot(ws_ref[gi], zv[r, sl].astype(BF16), preferred_element_type=F32)
            yb_ref[r, sl] = u[r, sl] * (s + gb_ref[:, sl])


def _pool_gmlp(mix, pool_w, pool_scale, gmlp_norm, ws_tril, gb_full):
    chunks = SEQ // POOL_ROWS
    row = lambda b, c: b * chunks + c
    full = lambda shape: pl.BlockSpec(shape, lambda b, c: (0,) * len(shape))
    return pl.pallas_call(
        _pool_gmlp_kernel,
        grid=(BATCH, chunks),
        in_specs=[
            pl.BlockSpec((POOL_ROWS, MIX_BLOCK), lambda b, c: (row(b, c), 0)),
            full((4, GROUP, GROUP)),
            full((1, POOL_WIDTH)),
            full((1, GMLP_WIDTH)),
            full((4, GMLP_CHUNK, GMLP_CHUNK)),
            full((GMLP_CHUNK, GMLP_WIDTH)),
        ],
        out_specs=[
            pl.BlockSpec((POOL_ROWS, POOL_WIDTH), lambda b, c: (row(b, c), 0)),
            pl.BlockSpec((POOL_ROWS, GMLP_WIDTH), lambda b, c: (row(b, c), 0)),
        ],
        out_shape=[
            jax.ShapeDtypeStruct((N_PROMPT, POOL_WIDTH), F32),
            jax.ShapeDtypeStruct((N_PROMPT, GMLP_WIDTH), F32),
        ],
        scratch_shapes=[pltpu.VMEM((16 + POOL_ROWS, POOL_WIDTH), F32)],
        compiler_params=pltpu.CompilerParams(
            dimension_semantics=("parallel", "arbitrary"),
            vmem_limit_bytes=VMEM_LIMIT),
        name="pool_gmlp",
    )(mix, pool_w, pool_scale, gmlp_norm, ws_tril, gb_full)


def _pair_bd(y, left):
    zero = jnp.zeros_like(y)
    return jnp.concatenate([jnp.where(left, y, zero), jnp.where(left, zero, y)],
                           axis=1).astype(BF16)


def _pair_mm(x, y_bd):
    return _bdot(x, y_bd)


def _pair_inverse(m, left, ri, ci, eye):
    t0 = jnp.broadcast_to(eye, m.shape)
    s = 1
    while s < DN_CHUNK:
        lower_left = ((ri // s) % 2 == 1) & ((ci // s) == (ri // s) - 1)
        tc = _pair_mm(t0, _pair_bd(jnp.where(lower_left, m, 0.0), left))
        t0 = t0 - _pair_mm(tc, _pair_bd(t0, left))
        s *= 2
    th, tl = _split(t0)
    mh, ml = _split(m)
    th_bd = _pair_bd(th, left)
    tl_bd = _pair_bd(tl, left)
    mt = _bdot(mh, th_bd) + (_bdot(mh, tl_bd) + _bdot(ml, th_bd))
    r = eye - t0 - mt
    return t0 + _bdot(th, _pair_bd(r, left))


def _cumsum_exact(g, tri, dims):
    g1 = g.astype(BF16)
    r1 = g - g1.astype(F32)
    g2 = r1.astype(BF16)
    g3 = (r1 - g2.astype(F32)).astype(BF16)
    if dims == 'rows':
        dot = lambda t: jnp.dot(tri, t, preferred_element_type=F32)
    else:
        dot = lambda t: jnp.dot(t, tri, preferred_element_type=F32)
    return dot(g1) + (dot(g2) + dot(g3))


def _delta_kernel(qk_ref, kv_ref, zb_ref, bat_ref, cw_ref, acol_ref, dtcol_ref,
                  arow_ref, dtrow_ref, on_ref, yc_ref, s_ref, ext_ref):
    n = pl.program_id(1)
    C = DN_CHUNK
    H = DN_HEADS
    R = DELTA_ROWS

    @pl.when(n == 0)
    def _():
        ext_ref[0:8, :] = jnp.zeros((8, CONV_DIM), F32)
        s_ref[...] = jnp.zeros(s_ref.shape, F32)

    ext_ref[8:8 + R, 0:MIX_BLOCK] = qk_ref[...]
    ext_ref[8:8 + R, MIX_BLOCK:CONV_DIM] = kv_ref[...]
    conv = cw_ref[3:4, :] * ext_ref[8:8 + R, :]
    for i in range(CONV_W - 1):
        conv = conv + cw_ref[i:i + 1, :] * ext_ref[5 + i:5 + i + R, :]
    ext_ref[0:8, :] = ext_ref[R:R + 8, :]
    act = _silu(conv)

    ri = lax.broadcasted_iota(jnp.int32, (C, 2 * C), 0)
    li = lax.broadcasted_iota(jnp.int32, (C, 2 * C), 1)
    ci = li % C
    left = li < C
    causal = ri >= ci
    strict = ri > ci
    eye = jnp.where(ri == ci, 1.0, 0.0).astype(F32)
    zero = jnp.zeros((C, DN_DK), F32)

    rr = lax.broadcasted_iota(jnp.int32, (R, R), 0)
    cc = lax.broadcasted_iota(jnp.int32, (R, R), 1)
    tril_chunks = jnp.where((rr // C == cc // C) & (rr >= cc), 1.0, 0.0).astype(BF16)
    pr = lax.broadcasted_iota(jnp.int32, (2 * C, 2 * C), 0)
    pc = lax.broadcasted_iota(jnp.int32, (2 * C, 2 * C), 1)
    triu_pair = jnp.where((pr // C == pc // C) & (pr <= pc), 1.0, 0.0).astype(BF16)

    ba = zb_ref[:, Z_WIDTH:Z_WIDTH + 128]
    beta_all = jax.nn.sigmoid(ba)
    g_all = -jnp.exp(acol_ref[...]) * _softplus(ba + dtcol_ref[...])
    gc_all = _cumsum_exact(g_all, tril_chunks, 'rows')
    on = on_ref[...]

    NC = R // C
    NP = H // 2
    lane2 = lambda a, b: jnp.concatenate([a, b], axis=1)
    diag2 = lambda a, b: jnp.concatenate([lane2(a, jnp.zeros_like(b)),
                                          lane2(jnp.zeros_like(a), b)], axis=0)

    ks, kbs, qes, gcs = [], [], [], []
    kq_lhs, kq_rhs, gc_pair, gr_pair, uw_rhs = [], [], [], [], []
    for c in range(NC):
        rows = slice(c * C, (c + 1) * C)
        heads = lambda base: jnp.stack(
            [act[rows, (base + h) * DN_DK:(base + h + 1) * DN_DK] for h in range(H)])
        q = heads(0)
        k = heads(H)
        v = heads(2 * H)
        q = q * lax.rsqrt(jnp.sum(q * q, axis=-1, keepdims=True) + EPS) * (DN_DK ** -0.5)
        k = k * lax.rsqrt(jnp.sum(k * k, axis=-1, keepdims=True) + EPS)
        beta = jnp.stack([beta_all[rows, h:h + 1] for h in range(H)])
        gc = jnp.stack([gc_all[rows, H + h:H + h + 1] for h in range(H)])
        egc = jnp.exp(gc)
        kb = k * beta
        vb = v * beta
        kbe = kb * egc
        ks.append(k)
        kbs.append(kb)
        qes.append(q * egc)
        gcs.append(gc)
        g_t = -jnp.exp(arow_ref[...]) * _softplus(bat_ref[c] + dtrow_ref[...])
        gr_all = _cumsum_exact(g_t, triu_pair, 'cols')
        for p in range(NP):
            a, b = 2 * p, 2 * p + 1
            kq_lhs.append(jnp.concatenate([lane2(kb[a], kb[b]), lane2(q[a], q[b])], axis=0))
            kq_rhs.append(diag2(k[a], k[b]))
            gc_pair.append(jnp.where(left, gc[a], gc[b]))
            gr_pair.append(gr_all[p:p + 1, :])
            uw_rhs.append(diag2(lane2(vb[a], kbe[a]), lane2(vb[b], kbe[b])))

    kq = jnp.einsum('bmk,bnk->bmn', jnp.stack(kq_lhs).astype(BF16), jnp.stack(kq_rhs).astype(BF16),
                    preferred_element_type=F32)
    diff = jnp.stack(gc_pair) - jnp.stack(gr_pair)
    decay = jnp.where(causal, jnp.exp(jnp.where(causal, diff, 0.0)), 0.0)
    m = jnp.where(strict, kq[:, 0:C] * decay, 0.0)
    a_mat = kq[:, C:2 * C] * decay
    t = _pair_inverse(m, left, ri, ci, eye)
    uw = _bdot(t, jnp.stack(uw_rhs))

    for c in range(NC):
        rows = slice(c * C, (c + 1) * C)
        part = lambda h, i: uw[c * NP + h // 2][:, (2 * (h % 2) + i) * DN_DV:
                                               (2 * (h % 2) + i + 1) * DN_DV]
        u = jnp.stack([part(h, 0) for h in range(H)])
        w = jnp.stack([part(h, 1) for h in range(H)])
        s = s_ref[0]
        ws_qs = _bdot(jnp.concatenate([w, qes[c]], axis=1), s)
        v_new = u - ws_qs[:, 0:C]
        glast = gcs[c][:, C - 1:C, :]
        kg = ks[c] * jnp.exp(glast - gcs[c])
        s_ref[0] = s * jnp.exp(glast) + _bdot_tn(kg, v_new)
        vn_bd = jnp.stack([diag2(v_new[2 * p], v_new[2 * p + 1]) for p in range(NP)])
        av = _bdot(a_mat[c * NP:(c + 1) * NP], vn_bd)
        o = ws_qs[:, C:2 * C] + jnp.stack(
            [av[h // 2][:, (h % 2) * DN_DV:(h % 2 + 1) * DN_DV] for h in range(H)])
        o = o * lax.rsqrt(jnp.mean(o * o, axis=-1, keepdims=True) + EPS) * on
        for h in range(H):
            sl = slice(h * DN_DV, (h + 1) * DN_DV)
            yc_ref[rows, sl] = o[h] * _silu(zb_ref[rows, sl])


def _delta(mix, bat, conv_w, acol, dtcol, arow, dtrow, onorm):
    steps = SEQ // DELTA_ROWS
    cps = DELTA_ROWS // DN_CHUNK
    row = lambda b, n: b * steps + n
    full = lambda shape: pl.BlockSpec(shape, lambda b, n: (0,) * len(shape))
    return pl.pallas_call(
        _delta_kernel,
        grid=(BATCH, steps),
        in_specs=[
            pl.BlockSpec((DELTA_ROWS, MIX_BLOCK), lambda b, n: (row(b, n), 1)),
            pl.BlockSpec((DELTA_ROWS, MIX_BLOCK), lambda b, n: (row(b, n), 2)),
            pl.BlockSpec((DELTA_ROWS, MIX_BLOCK), lambda b, n: (row(b, n), 3)),
            pl.BlockSpec((cps, 8, 2 * DN_CHUNK), lambda b, n: (row(b, n), 0, 0)),
            full((CONV_W, CONV_DIM)),
            full((1, 128)),
            full((1, 128)),
            full((8, 2 * DN_CHUNK)),
            full((8, 2 * DN_CHUNK)),
            full((1, DN_DV)),
        ],
        out_specs=[
            pl.BlockSpec((DELTA_ROWS, Z_WIDTH), lambda b, n: (row(b, n), 0)),
            pl.BlockSpec((1, DN_HEADS, DN_DK, DN_DV), lambda b, n: (b, 0, 0, 0)),
        ],
        out_shape=[
            jax.ShapeDtypeStruct((N_PROMPT, Z_WIDTH), F32),
            jax.ShapeDtypeStruct((BATCH, DN_HEADS, DN_DK, DN_DV), F32),
        ],
        scratch_shapes=[pltpu.VMEM((8 + DELTA_ROWS, CONV_DIM), F32)],
        compiler_params=pltpu.CompilerParams(
            dimension_semantics=("parallel", "arbitrary"),
            vmem_limit_bytes=VMEM_LIMIT),
        name="delta",
    )(mix, mix, mix, bat, conv_w, acol, dtcol, arow, dtrow, onorm)


def _sample_rows_kernel(m_ref, qk_ref, kv_ref, zb_ref, pool_ref, cst_ref,
                        pw_ref, ps_ref, gn_ref, wsd_ref, gb0_ref, cw_ref, acol_ref, dtcol_ref,
                        ya_ref, yb_ref, zv_ref, npool_ref, nconv_ref, vec_ref, eg_ref, qk_out):
    H = DN_HEADS
    a = m_ref[:, 0:POOL_WIDTH]
    for gi, w in enumerate(POOL_WINDOWS):
        sl = slice(gi * GROUP, (gi + 1) * GROUP)
        s = a[:, sl]
        for i in range(1, w):
            s = s + pool_ref[POOL_HIST - i, :, sl]
        d = s / float(w) - a[:, sl]
        y = jnp.dot(d.astype(BF16), pw_ref[gi], preferred_element_type=F32)
        ya_ref[:, sl] = y * ps_ref[:, sl]
    for i in range(POOL_HIST - 1):
        npool_ref[i] = pool_ref[i + 1]
    npool_ref[POOL_HIST - 1] = a

    u, zv = _gmlp_gate(m_ref[:, POOL_WIDTH:POOL_WIDTH + GMLP_WIDTH],
                       m_ref[:, POOL_WIDTH + GMLP_WIDTH:MIX_BLOCK], gn_ref[...])
    zv_ref[...] = zv
    yb_ref[...] = u * (wsd_ref[...] * zv + gb0_ref[...])

    x = jnp.concatenate([qk_ref[...], kv_ref[...]], axis=-1)
    conv = cw_ref[3:4, :] * x
    for i in range(CONV_W - 1):
        conv = conv + cw_ref[i:i + 1, :] * cst_ref[i]
    nconv_ref[0] = cst_ref[1]
    nconv_ref[1] = cst_ref[2]
    nconv_ref[2] = x
    act = _silu(conv)
    ba = zb_ref[:, Z_WIDTH:Z_WIDTH + 128]
    beta_all = jax.nn.sigmoid(ba)
    g_all = -jnp.exp(acol_ref[...]) * _softplus(ba + dtcol_ref[...])
    eg_all = jnp.exp(g_all)
    eg_ref[...] = eg_all
    lane = lax.broadcasted_iota(jnp.int32, (DEC_BATCH, 128), 1)
    qk_all = jnp.zeros((DEC_BATCH, 128), F32)
    for h in range(H):
        q = act[:, h * DN_DK:(h + 1) * DN_DK]
        k = act[:, (H + h) * DN_DK:(H + h + 1) * DN_DK]
        v = act[:, (2 * H + h) * DN_DK:(2 * H + h + 1) * DN_DK]
        q = q * lax.rsqrt(jnp.sum(q * q, axis=-1, keepdims=True) + EPS) * (DN_DK ** -0.5)
        k = k * lax.rsqrt(jnp.sum(k * k, axis=-1, keepdims=True) + EPS)
        beta = beta_all[:, h:h + 1]
        eg = eg_all[:, H + h:H + h + 1]
        vec_ref[:, h * 128:(h + 1) * 128] = k * (beta * eg)
        vec_ref[:, (H + h) * 128:(H + h + 1) * 128] = q * eg
        vec_ref[:, (2 * H + h) * 128:(2 * H + h + 1) * 128] = k
        vec_ref[:, (3 * H + h) * 128:(3 * H + h + 1) * 128] = v * beta
        qk = jnp.sum(q * k, axis=-1, keepdims=True)
        qk_all = jnp.where(lane == h, qk, qk_all)
    qk_out[...] = qk_all


def _sample_rows(mix, pool_t, conv_t, pool_w, pool_scale, gmlp_norm, ws_diag, gb0, conv_w,
                 acol, dtcol):
    rb = N_PROMPT // DEC_BATCH
    full = lambda shape: pl.BlockSpec(shape, lambda i: (0,) * len(shape))
    n = DEC_BATCH
    return pl.pallas_call(
        _sample_rows_kernel,
        grid=(1,),
        in_specs=[
            pl.BlockSpec((n, MIX_BLOCK), lambda i: (rb, 0)),
            pl.BlockSpec((n, MIX_BLOCK), lambda i: (rb, 1)),
            pl.BlockSpec((n, MIX_BLOCK), lambda i: (rb, 2)),
            pl.BlockSpec((n, MIX_BLOCK), lambda i: (rb, 3)),
            full((POOL_HIST, n, POOL_WIDTH)),
            full((CONV_W - 1, n, CONV_DIM)),
            full((4, GROUP, GROUP)),
            full((1, POOL_WIDTH)),
            full((1, GMLP_WIDTH)),
            full((1, GMLP_WIDTH)),
            full((1, GMLP_WIDTH)),
            full((CONV_W, CONV_DIM)),
            full((1, 128)),
            full((1, 128)),
        ],
        out_specs=[
            full((n, POOL_WIDTH)),
            full((n, GMLP_WIDTH)),
            full((n, GMLP_WIDTH)),
            full((POOL_HIST, n, POOL_WIDTH)),
            full((CONV_W - 1, n, CONV_DIM)),
            full((n, 4 * DN_HEADS * 128)),
            full((n, 128)),
            full((n, 128)),
        ],
        out_shape=[
            jax.ShapeDtypeStruct((n, POOL_WIDTH), F32),
            jax.ShapeDtypeStruct((n, GMLP_WIDTH), F32),
            jax.ShapeDtypeStruct((n, GMLP_WIDTH), F32),
            jax.ShapeDtypeStruct((POOL_HIST, n, POOL_WIDTH), F32),
            jax.ShapeDtypeStruct((CONV_W - 1, n, CONV_DIM), F32),
            jax.ShapeDtypeStruct((n, 4 * DN_HEADS * 128), F32),
            jax.ShapeDtypeStruct((n, 128), F32),
            jax.ShapeDtypeStruct((n, 128), F32),
        ],
        compiler_params=pltpu.CompilerParams(
            dimension_semantics=("arbitrary",),
            vmem_limit_bytes=VMEM_LIMIT),
        name="sample_rows",
    )(mix, mix, mix, mix, pool_t, conv_t, pool_w, pool_scale, gmlp_norm,
      ws_diag, gb0, conv_w, acol, dtcol)


def _sample_state_kernel(vec_ref, scal_ref, z_ref, on_ref, s_ref, *rest):
    yc_ref, so_ref = rest[-2:]
    H = DN_HEADS
    row = lax.broadcasted_iota(jnp.int32, (1, 8, 1), 1)
    for h in range(H):
        s = s_ref[0, :, h]
        w = vec_ref[:, h:h + 1, :]
        qg = vec_ref[:, H + h:H + h + 1, :]
        k = vec_ref[:, 2 * H + h:2 * H + h + 1, :]
        vb = vec_ref[:, 3 * H + h:3 * H + h + 1, :]
        eg = scal_ref[:, h:h + 1, :]
        qk = scal_ref[:, H + h:H + h + 1, :]
        lhs = jnp.where(row == 0, w, jnp.where(row == 1, qg, 0.0))
        r = _bdot(lhs, s)
        v_new = vb - r[:, 0:1, :]
        o = r[:, 1:2, :] + qk * v_new
        k8 = jnp.where(row == 0, k, 0.0)
        v8 = jnp.broadcast_to(v_new, k8.shape)
        so_ref[0, :, h] = s * eg + _bdot_tn(k8, v8)
        o = o * lax.rsqrt(jnp.mean(o * o, axis=-1, keepdims=True) + EPS) * on_ref[...]
        yc_ref[:, h:h + 1, :] = o * _silu(z_ref[:, h:h + 1, :])


def _sample_state(layer, vec3, scal3, z3, onorm, state, stacked):
    sb = SAMPLE_BLOCK
    H = DN_HEADS
    in_specs = [
        pl.BlockSpec((sb, 4 * H, 128), lambda i: (i, 0, 0)),
        pl.BlockSpec((sb, 2 * H, 1), lambda i: (i, 0, 0)),
        pl.BlockSpec((sb, H, DN_DV), lambda i: (i, 0, 0)),
        pl.BlockSpec((1, DN_DV), lambda i: (0, 0)),
        pl.BlockSpec((1, sb, H, DN_DK, DN_DV), lambda i: (layer, i, 0, 0, 0)),
    ]
    args = [vec3, scal3, z3, onorm, state]
    aliases = {}
    if stacked is not None:
        in_specs.append(pl.BlockSpec(memory_space=pl.ANY))
        args.append(stacked)
        aliases = {5: 1}
    return pl.pallas_call(
        _sample_state_kernel,
        grid=(DEC_BATCH // sb,),
        in_specs=in_specs,
        out_specs=[
            pl.BlockSpec((sb, H, DN_DV), lambda i: (i, 0, 0)),
            pl.BlockSpec((1, sb, H, DN_DK, DN_DV), lambda i: (layer, i, 0, 0, 0)),
        ],
        out_shape=[
            jax.ShapeDtypeStruct((DEC_BATCH, H, DN_DV), F32),
            jax.ShapeDtypeStruct((DEPTH, DEC_BATCH, H, DN_DK, DN_DV), F32),
        ],
        input_output_aliases=aliases,
        compiler_params=pltpu.CompilerParams(
            dimension_semantics=("parallel",),
            vmem_limit_bytes=VMEM_LIMIT),
        name="sample_state",
    )(*args)


def _merge_kernel(x_ref, g0_ref, g1_ref, g2_ref, bg_ref, ya_ref, yb_ref, yc_ref,
                  pa_ref, pb_ref, pc_ref, wo_ref, *rest):
    o_ref = rest[-1]
    lane = lax.broadcasted_iota(jnp.int32, (1, GATE_WIN), 1)
    valid = (lane >= GATE_SHIFT) & (lane < GATE_SHIFT + D_MODEL)

    def branch(g_ref, i, y_ref, p_ref):
        gate = jax.nn.sigmoid(g_ref[...] + bg_ref[:, i * GATE_WIN:(i + 1) * GATE_WIN])
        gate = jnp.where(valid, gate, 0.0)
        return gate * jnp.dot(y_ref[...].astype(BF16), p_ref[...], preferred_element_type=F32)

    m = branch(g0_ref, 0, ya_ref, pa_ref)
    m = m + branch(g1_ref, 1, yb_ref, pb_ref)
    m = m + branch(g2_ref, 2, yc_ref, pc_ref)
    o_ref[...] = x_ref[...] + jnp.dot(m.astype(BF16), wo_ref[...], preferred_element_type=F32)


def _shift_cols_kernel(w_ref, o_ref):
    w = w_ref[...]
    wide = jnp.concatenate([w, jnp.zeros((w.shape[0], GATE_WIN - D_MODEL), F32)], axis=1)
    o_ref[...] = pltpu.roll(wide, GATE_SHIFT, axis=1).astype(BF16)


def _shift_cols(w):
    depth, k, _ = w.shape
    rows = 512
    return pl.pallas_call(
        _shift_cols_kernel,
        grid=(depth, k // rows),
        in_specs=[pl.BlockSpec((None, rows, D_MODEL), lambda l, r: (l, r, 0))],
        out_specs=pl.BlockSpec((None, rows, GATE_WIN), lambda l, r: (l, r, 0)),
        out_shape=jax.ShapeDtypeStruct((depth, k, GATE_WIN), BF16),
        compiler_params=pltpu.CompilerParams(dimension_semantics=("parallel", "parallel"),
                                             vmem_limit_bytes=VMEM_LIMIT),
        name="shift_cols",
    )(w)


def _shift_rows_kernel(w_ref, o_ref):
    tail = GATE_WIN - D_MODEL - GATE_SHIFT
    o_ref[0:GATE_SHIFT, :] = jnp.zeros((GATE_SHIFT, D_MODEL), BF16)
    o_ref[GATE_SHIFT:GATE_SHIFT + D_MODEL, :] = w_ref[...].astype(BF16)
    o_ref[GATE_SHIFT + D_MODEL:GATE_WIN, :] = jnp.zeros((tail, D_MODEL), BF16)


def _shift_rows(w):
    depth = w.shape[0]
    return pl.pallas_call(
        _shift_rows_kernel,
        grid=(depth,),
        in_specs=[pl.BlockSpec((None, D_MODEL, D_MODEL), lambda l: (l, 0, 0))],
        out_specs=pl.BlockSpec((None, GATE_WIN, D_MODEL), lambda l: (l, 0, 0)),
        out_shape=jax.ShapeDtypeStruct((depth, GATE_WIN, D_MODEL), BF16),
        compiler_params=pltpu.CompilerParams(dimension_semantics=("parallel",),
                                             vmem_limit_bytes=VMEM_LIMIT),
        name="shift_rows",
    )(w)


def _merge(layer, x, proj, b_gate, ya, yb, yc, pa, pb, pc, wo, tile, first, count, prev):
    once = lambda shape: pl.BlockSpec(shape, lambda i: (0,) * len(shape),
                                      pipeline_mode=pl.Buffered(1))
    slab = lambda k, n: pl.BlockSpec((None, k, n), lambda i: (layer, 0, 0),
                                     pipeline_mode=pl.Buffered(1))
    rows = lambda width: pl.BlockSpec((tile, width), lambda i: (first + i, 0))
    local = lambda width: pl.BlockSpec((tile, width), lambda i: (i, 0))
    gate = lambda k: pl.BlockSpec((pl.Element(tile), pl.Element(GATE_WIN)),
                                  lambda i: ((first + i) * tile, COL_BA + k * D_MODEL))
    in_specs = [
        rows(D_MODEL),
        gate(0), gate(1), gate(2),
        once((1, 3 * GATE_WIN)),
        local(POOL_WIDTH), local(GMLP_WIDTH), local(Z_WIDTH),
        slab(POOL_WIDTH, GATE_WIN),
        slab(GMLP_WIDTH, GATE_WIN),
        slab(Z_WIDTH, GATE_WIN),
        slab(GATE_WIN, D_MODEL),
    ]
    args = [x, proj, proj, proj, b_gate, ya, yb, yc, pa, pb, pc, wo]
    aliases = {}
    if prev is not None:
        in_specs.append(pl.BlockSpec(memory_space=pl.ANY))
        args.append(prev)
        aliases = {len(args) - 1: 0}
    return pl.pallas_call(
        _merge_kernel,
        grid=(count,),
        in_specs=in_specs,
        out_specs=rows(D_MODEL),
        out_shape=jax.ShapeDtypeStruct((N_ROWS, D_MODEL), F32),
        input_output_aliases=aliases,
        compiler_params=pltpu.CompilerParams(
            dimension_semantics=("parallel",),
            vmem_limit_bytes=VMEM_LIMIT),
        name="merge",
    )(*args)


def _final_kernel(x_ref, g_ref, o_ref):
    o_ref[...] = _rmsnorm(x_ref[...], g_ref[...])


def _final_norm(x, g, t, first, count):
    return pl.pallas_call(
        _final_kernel,
        grid=(count,),
        in_specs=[pl.BlockSpec((t, D_MODEL), lambda i: (first + i, 0)),
                  pl.BlockSpec((1, D_MODEL), lambda i: (0, 0))],
        out_specs=pl.BlockSpec((t, D_MODEL), lambda i: (i, 0)),
        out_shape=jax.ShapeDtypeStruct((count * t, D_MODEL), F32),
        compiler_params=pltpu.CompilerParams(dimension_semantics=("parallel",),
                                             vmem_limit_bytes=VMEM_LIMIT),
        name="final_norm",
    )(x, g)


def _lane_pad(v, offset, width=128):
    return jnp.zeros((1, width), F32).at[0, offset:offset + v.shape[0]].set(v)


def _tail_rows(mix, count, col, width):
    return jnp.stack([
        lax.slice(mix, ((b + 1) * SEQ - count, col), ((b + 1) * SEQ, col + width))
        for b in range(BATCH)])


def kernel(x_prompt, x_sample, state_delta, state_conv, state_pool, ffn1_norm, ffn1_wg, ffn1_wu, ffn1_wd, mix_norm, w_in, b_gate, pool_w, pool_scale, gmlp_norm, gmlp_ws, gmlp_b, dn_conv, dn_a_log, dn_dt_bias, dn_onorm, proj_a, proj_b, proj_c, w_o, ffn2_norm, ffn2_wg, ffn2_wu, ffn2_wd, final_norm):
    H = DN_HEADS
    x = jnp.concatenate([x_prompt.reshape(N_PROMPT, D_MODEL),
                         x_sample.reshape(DEC_BATCH, D_MODEL)], axis=0)
    causal = jnp.tril(jnp.ones((GMLP_CHUNK, GMLP_CHUNK), bool))
    outs_delta_p, outs_conv_p, outs_conv_s = [], [], []
    outs_pool_p, outs_pool_s, outs_zv = [], [], []
    delta_s = None
    w_in_t = jnp.swapaxes(w_in, 1, 2)
    weights = (_shift_cols(proj_a), _shift_cols(proj_b), _shift_cols(proj_c), _shift_rows(w_o))
    shift_cols = ((0, 0), (GATE_SHIFT, GATE_WIN - D_MODEL - GATE_SHIFT))
    for l in range(DEPTH):
        x, h_mix = _ffn(l, x, ffn1_norm[l][None], ffn1_wg, ffn1_wu, ffn1_wd,
                        next_gain=mix_norm[l][None])

        mix = _inproj(l, h_mix, w_in_t)

        pw = pool_w[l].astype(BF16)
        ps = pool_scale[l][None]
        gn = gmlp_norm[l][None]
        ws_tril = jnp.where(causal, gmlp_ws[l], 0).astype(BF16)
        gb_full = jnp.repeat(gmlp_b[l].T, GROUP, axis=1)
        ws_diag = jnp.repeat(gmlp_ws[l][:, 0, 0], GROUP)[None]
        gb0 = jnp.repeat(gmlp_b[l][:, 0], GROUP)[None]
        acol = _lane_pad(dn_a_log[l], H)
        dtcol = _lane_pad(dn_dt_bias[l], H)
        pair_rows = lambda v: jnp.pad(jnp.repeat(v.reshape(H // 2, 2), DN_CHUNK, axis=1),
                                      ((0, 8 - H // 2), (0, 0)))
        arow = pair_rows(dn_a_log[l])
        dtrow = pair_rows(dn_dt_bias[l])
        onorm = dn_onorm[l][None]

        ya_p, yb_p = _pool_gmlp(mix, pw, ps, gn, ws_tril, gb_full)
        a_p = lax.slice(mix, (0, COL_BA + H), (N_PROMPT, COL_BA + 2 * H))
        bat = a_p.reshape(N_PROMPT // DN_CHUNK, DN_CHUNK, H // 2, 2).transpose(0, 2, 3, 1)
        bat = jnp.pad(bat.reshape(N_PROMPT // DN_CHUNK, H // 2, 2 * DN_CHUNK),
                      ((0, 0), (0, 8 - H // 2), (0, 0)))
        yc_p, s_p = _delta(mix, bat, dn_conv[l], acol, dtcol, arow, dtrow, onorm)
        outs_delta_p.append(s_p)
        outs_conv_p.append(_tail_rows(mix, CONV_W - 1, COL_QKV, CONV_DIM))
        outs_pool_p.append(_tail_rows(mix, POOL_HIST, 0, POOL_WIDTH))

        pool_t = state_pool[l].transpose(1, 0, 2)
        conv_t = state_conv[l].transpose(1, 0, 2)
        ya_s, yb_s, zv_s, npool_t, nconv_t, vec, eg_all, qk_all = _sample_rows(
            mix, pool_t, conv_t, pw, ps, gn, ws_diag, gb0, dn_conv[l], acol, dtcol)
        vec3 = vec.reshape(DEC_BATCH, 4 * H, 128)
        scal3 = jnp.concatenate([eg_all[:, H:2 * H], qk_all[:, :H]], axis=1)[:, :, None]
        z3 = lax.slice(mix, (N_PROMPT, COL_Z), (N_ROWS, COL_Z + Z_WIDTH)).reshape(DEC_BATCH, H, DN_DV)
        yc_s3, delta_s = _sample_state(l, vec3, scal3, z3, onorm, state_delta, delta_s)
        outs_conv_s.append(nconv_t.transpose(1, 0, 2))
        outs_pool_s.append(npool_t.transpose(1, 0, 2))
        outs_zv.append(zv_s[:, None, :])

        bg = jnp.pad(b_gate[l].reshape(3, D_MODEL), shift_cols).reshape(1, 3 * GATE_WIN)
        x_new = _merge(l, x, mix, bg, ya_p, yb_p, yc_p, *weights,
                       tile=MERGE_TILE, first=0, count=N_PROMPT // MERGE_TILE, prev=None)
        x = _merge(l, x, mix, bg, ya_s, yb_s, yc_s3.reshape(DEC_BATCH, Z_WIDTH), *weights,
                   tile=DEC_BATCH, first=N_PROMPT // DEC_BATCH, count=1, prev=x_new)

        x = _ffn(l, x, ffn2_norm[l][None], ffn2_wg, ffn2_wu, ffn2_wd)

    g = final_norm[None]
    y_prompt = _final_norm(x, g, NORM_TILE, 0, N_PROMPT // NORM_TILE).reshape(BATCH, SEQ, D_MODEL)
    y_sample = _final_norm(x, g, DEC_BATCH, N_PROMPT // DEC_BATCH, 1).reshape(
        DEC_BATCH, 1, D_MODEL)
    return (y_prompt, y_sample, jnp.stack(outs_delta_p), delta_s,
            jnp.stack(outs_conv_p), jnp.stack(outs_conv_s), jnp.stack(outs_pool_p),
            jnp.stack(outs_pool_s), jnp.stack(outs_zv))
```
